```python
import jax, jax.numpy as jnp
from jax import lax
import numpy as np

D_MODEL = 1024
BATCH = 2
SEQ = 8192
DEPTH = 1

CHUNK = 64
SGU_BLOCK = 128
SGU_GROUPS = 8
SGU_GROUP_DIM = D_MODEL // SGU_GROUPS
SGU_WIDTH = SGU_GROUPS * SGU_GROUP_DIM
GDN_HEADS = 8
GDN_HEAD_DIM = D_MODEL // GDN_HEADS
GDN_WIDTH = GDN_HEADS * GDN_HEAD_DIM
CONV_WIDTH = 4
N_BRANCH = 2
D_FF = 2816
N_SUB = 3
MACARON_WEIGHT = 0.5
DEEPNORM_ALPHA = (2.0 * DEPTH) ** 0.25
DEEPNORM_BETA = (8.0 * DEPTH) ** -0.25
LN_EPS = 1e-5
RMS_EPS = 1e-6
L2_EPS = 1e-6
ADA_SCALE = 0.1
PROJ_SIZES = (SGU_WIDTH, SGU_WIDTH, 3 * GDN_WIDTH, GDN_WIDTH, GDN_HEADS, GDN_HEADS, N_BRANCH * D_MODEL)
PROJ_COLS = SGU_WIDTH * 2 + GDN_WIDTH * 4 + GDN_HEADS * 2 + N_BRANCH * D_MODEL

kernel_name = "hybrid_sgu_gdn_macaron_deepnorm_adaln"


def _split_points(sizes):
    points, acc = [], 0
    for s in sizes[:-1]:
        acc += s
        points.append(acc)
    return points


def layer_norm(x, gain, bias):
    xf = x.astype(jnp.float32)
    mu = jnp.mean(xf, axis=-1, keepdims=True)
    var = jnp.mean(jnp.square(xf - mu), axis=-1, keepdims=True)
    return ((xf - mu) * lax.rsqrt(var + LN_EPS) * gain.astype(jnp.float32) + bias.astype(jnp.float32)).astype(x.dtype)


def modulate(x, shift, scale):
    return x * (1 + scale[:, None, :]) + shift[:, None, :]


def swiglu(h, w_gate, w_up, w_down):
    return (jax.nn.silu(h @ w_gate) * (h @ w_up)) @ w_down


def causal_depthwise_conv(x, w):
    k = w.shape[0]
    return lax.conv_general_dilated(x, w[:, None, :], window_strides=(1,), padding=((k - 1, 0),),
                                    dimension_numbers=("NWC", "WIO", "NWC"),
                                    feature_group_count=x.shape[-1])


def spatial_gating(u, v, ln_g, ln_b, w_s, b_s):
    bsz, seq, _ = v.shape
    v = layer_norm(v, ln_g, ln_b)
    n = seq // SGU_BLOCK
    vb = v.reshape(bsz, n, SGU_BLOCK, SGU_GROUPS, SGU_GROUP_DIM)
    chunk_id = jnp.arange(SGU_BLOCK) // CHUNK
    mask = chunk_id[:, None] >= chunk_id[None, :]
    w = jnp.where(mask[None], w_s, 0)
    mixed = jnp.einsum("gts,bnsgc->bntgc", w, vb) + b_s.T[None, None, :, :, None]
    return u * mixed.reshape(bsz, seq, SGU_WIDTH)


def l2norm(t):
    return t * lax.rsqrt(jnp.sum(jnp.square(t), axis=-1, keepdims=True) + L2_EPS)


def gated_delta_rule(q, k, v, g, beta):
    bsz, h, seq, dk = q.shape
    dv = v.shape[-1]
    n = seq // CHUNK
    q = q * (dk ** -0.5)

    def chunks(t):
        return t.reshape((bsz, h, n, CHUNK) + t.shape[3:])

    q, k, v, g, beta = chunks(q), chunks(k), chunks(v), chunks(g), chunks(beta)
    g = jnp.cumsum(g, axis=-1)
    k_beta = k * beta[..., None]
    v_beta = v * beta[..., None]
    idx = jnp.arange(CHUNK)
    causal = idx[:, None] >= idx[None, :]
    strict = idx[:, None] > idx[None, :]
    decay = jnp.exp(jnp.where(causal, g[..., :, None] - g[..., None, :], -jnp.inf))
    a_mat = jnp.where(strict, jnp.einsum("bhnid,bhnjd->bhnij", k_beta, k) * decay, 0.0)
    lhs = a_mat + jnp.eye(CHUNK, dtype=jnp.float32)
    u_val = lax.linalg.triangular_solve(lhs, v_beta, left_side=True, lower=True, unit_diagonal=True)
    w_key = lax.linalg.triangular_solve(lhs, k_beta * jnp.exp(g)[..., None], left_side=True, lower=True,
                                        unit_diagonal=True)
    qk = jnp.where(causal, jnp.einsum("bhnid,bhnjd->bhnij", q, k) * decay, 0.0)
    q_decayed = q * jnp.exp(g)[..., None]
    k_to_end = k * jnp.exp(g[..., -1:] - g)[..., None]
    chunk_decay = jnp.exp(g[..., -1])
    xs = tuple(jnp.moveaxis(t, 2, 0) for t in (qk, q_decayed, u_val, w_key, k_to_end, chunk_decay))

    def step(state, inp):
        qk_c, qd_c, u_c, w_c, kend_c, cd_c = inp
        v_new = u_c - jnp.einsum("bhcd,bhde->bhce", w_c, state)
        out = jnp.einsum("bhcd,bhde->bhce", qd_c, state) + jnp.einsum("bhij,bhje->bhie", qk_c, v_new)
        state = state * cd_c[..., None, None] + jnp.einsum("bhcd,bhce->bhde", kend_c, v_new)
        return state, out

    state0 = jnp.zeros((bsz, h, dk, dv), jnp.float32)
    _, out = lax.scan(step, state0, xs)
    return jnp.moveaxis(out, 0, 2).reshape(bsz, h, seq, dv)


def gdn_branch(qkv, z, a, b, conv_w, a_log, dt_bias, norm_w):
    bsz, seq, _ = z.shape
    f32 = jnp.float32
    qkv = jax.nn.silu(causal_depthwise_conv(qkv, conv_w))
    q, k, v = jnp.split(qkv, 3, axis=-1)

    def heads(t):
        return t.reshape(bsz, seq, GDN_HEADS, GDN_HEAD_DIM).transpose(0, 2, 1, 3).astype(f32)

    q, k, v = l2norm(heads(q)), l2norm(heads(k)), heads(v)
    beta = jax.nn.sigmoid(b.astype(f32)).transpose(0, 2, 1)
    g = (-jnp.exp(a_log.astype(f32)) * jax.nn.softplus(a.astype(f32) + dt_bias.astype(f32))).transpose(0, 2, 1)
    o = gated_delta_rule(q, k, v, g, beta).transpose(0, 2, 1, 3)
    o = o * lax.rsqrt(jnp.mean(jnp.square(o), axis=-1, keepdims=True) + RMS_EPS) * norm_w.astype(f32)
    o = o * jax.nn.silu(z.reshape(bsz, seq, GDN_HEADS, GDN_HEAD_DIM).astype(f32))
    return o.reshape(bsz, seq, GDN_WIDTH).astype(z.dtype)


def hybrid_mixer(h, w_in, sgu_ln_g, sgu_ln_b, sgu_w_s, sgu_b_s, conv_w, a_log, dt_bias, norm_w, w_out):
    proj = h @ w_in
    u, v_s, qkv, z, a, b, gates = jnp.split(proj, _split_points(PROJ_SIZES), axis=-1)
    y_a = spatial_gating(jax.nn.gelu(u), jax.nn.gelu(v_s), sgu_ln_g, sgu_ln_b, sgu_w_s, sgu_b_s)
    y_b = gdn_branch(qkv, z, a, b, conv_w, a_log, dt_bias, norm_w)
    gate_a, gate_b = jnp.split(jax.nn.sigmoid(gates), N_BRANCH, axis=-1)
    return (gate_a * y_a + gate_b * y_b) @ w_out


def setup_inputs(seed: int = 0) -> dict:
    key = jax.random.key(seed)
    ks = jax.random.split(key, 32)
    f32 = jnp.float32

    def nrm(k, shape, scale):
        return jax.random.normal(k, shape, f32) * scale

    L, D = DEPTH, D_MODEL
    dt = jnp.exp(jax.random.uniform(ks[13], (L, GDN_HEADS), f32, np.log(1e-3), np.log(1e-1)))
    return {
        "x": nrm(ks[0], (BATCH, SEQ, D), 1.0),
        "c": nrm(ks[1], (BATCH, D), 1.0),
        "w_ada": nrm(ks[2], (L, D, 3 * N_SUB * D), ADA_SCALE * D ** -0.5),
        "b_ada": nrm(ks[3], (L, 3 * N_SUB * D), 0.01),
        "ffn1_w_gate": nrm(ks[4], (L, D, D_FF), D ** -0.5),
        "ffn1_w_up": nrm(ks[5], (L, D, D_FF), D ** -0.5),
        "ffn1_w_down": nrm(ks[6], (L, D_FF, D), DEEPNORM_BETA * D_FF ** -0.5),
        "ln1_g": 1.0 + nrm(ks[7], (L, D), 0.02),
        "ln1_b": nrm(ks[8], (L, D), 0.02),
        "mix_w_in": nrm(ks[9], (L, D, PROJ_COLS), D ** -0.5),
        "sgu_ln_g": 1.0 + nrm(ks[10], (L, SGU_WIDTH), 0.02),
        "sgu_ln_b": nrm(ks[11], (L, SGU_WIDTH), 0.02),
        "sgu_w_s": nrm(ks[12], (L, SGU_GROUPS, SGU_BLOCK, SGU_BLOCK), SGU_BLOCK ** -0.5),
        "sgu_b_s": 1.0 + nrm(ks[14], (L, SGU_GROUPS, SGU_BLOCK), 0.02),
        "gdn_conv_w": nrm(ks[15], (L, CONV_WIDTH, 3 * GDN_WIDTH), CONV_WIDTH ** -0.5),
        "gdn_a_log": jnp.log(jax.random.uniform(ks[16], (L, GDN_HEADS), f32, 1.0, 16.0)),
        "gdn_dt_bias": jnp.log(jnp.expm1(dt)),
        "gdn_norm_w": 1.0 + nrm(ks[17], (L, GDN_HEAD_DIM), 0.02),
        "mix_w_out": nrm(ks[18], (L, D, D), DEEPNORM_BETA * D ** -0.5),
        "ln2_g": 1.0 + nrm(ks[19], (L, D), 0.02),
        "ln2_b": nrm(ks[20], (L, D), 0.02),
        "ffn2_w_gate": nrm(ks[21], (L, D, D_FF), D ** -0.5),
        "ffn2_w_up": nrm(ks[22], (L, D, D_FF), D ** -0.5),
        "ffn2_w_down": nrm(ks[23], (L, D_FF, D), DEEPNORM_BETA * D_FF ** -0.5),
        "ln3_g": 1.0 + nrm(ks[24], (L, D), 0.02),
        "ln3_b": nrm(ks[25], (L, D), 0.02),
    }


def reference(x, c, w_ada, b_ada, ffn1_w_gate, ffn1_w_up, ffn1_w_down, ln1_g, ln1_b,
              mix_w_in, sgu_ln_g, sgu_ln_b, sgu_w_s, sgu_b_s, gdn_conv_w, gdn_a_log, gdn_dt_bias,
              gdn_norm_w, mix_w_out, ln2_g, ln2_b, ffn2_w_gate, ffn2_w_up, ffn2_w_down, ln3_g, ln3_b):
    cond = jax.nn.silu(c)
    for l in range(DEPTH):
        ada = cond @ w_ada[l] + b_ada[l]
        sh1, sc1, gt1, sh2, sc2, gt2, sh3, sc3, gt3 = jnp.split(ada, 3 * N_SUB, axis=-1)
        f1 = swiglu(modulate(x, sh1, sc1), ffn1_w_gate[l], ffn1_w_up[l], ffn1_w_down[l])
        x = layer_norm(DEEPNORM_ALPHA * x + MACARON_WEIGHT * (1 + gt1)[:, None, :] * f1, ln1_g[l], ln1_b[l])
        y = hybrid_mixer(modulate(x, sh2, sc2), mix_w_in[l], sgu_ln_g[l], sgu_ln_b[l], sgu_w_s[l], sgu_b_s[l],
                         gdn_conv_w[l], gdn_a_log[l], gdn_dt_bias[l], gdn_norm_w[l], mix_w_out[l])
        x = layer_norm(DEEPNORM_ALPHA * x + (1 + gt2)[:, None, :] * y, ln2_g[l], ln2_b[l])
        f2 = swiglu(modulate(x, sh3, sc3), ffn2_w_gate[l], ffn2_w_up[l], ffn2_w_down[l])
        x = layer_norm(DEEPNORM_ALPHA * x + MACARON_WEIGHT * (1 + gt3)[:, None, :] * f2, ln3_g[l], ln3_b[l])
    return x
```

```python
import functools

import jax
import jax.numpy as jnp
from jax import lax
from jax.experimental import pallas as pl
from jax.experimental.pallas import tpu as pltpu

F32 = jnp.float32
BF16 = jnp.bfloat16

N_SUB = 3
N_HEADS = 8
HEAD_DIM = 128
SGU_GROUPS = 8
SGU_BLOCK = 128
SGU_CHUNK = 64
CONV_WIDTH = 4
DEPTH = 1
MACARON_WEIGHT = 0.5
DEEPNORM_ALPHA = (2.0 * DEPTH) ** 0.25
LN_EPS = 1e-5
RMS_EPS = 1e-6
L2_EPS = 1e-6

LANES = 128
SUBLANES = 8
VMEM_LIMIT_BYTES = 56 * 1024 * 1024

MIX_TILE = 128
NEUMANN_BLOCK = 16
FFN_TILE = 256


def _dot(a, b):
    return jnp.dot(a.astype(BF16), b.astype(BF16), preferred_element_type=F32)


def _layer_norm(y, gain, bias):
    mu = jnp.mean(y, axis=-1, keepdims=True)
    yc = y - mu
    var = jnp.mean(yc * yc, axis=-1, keepdims=True)
    return yc * lax.rsqrt(var + LN_EPS) * gain + bias


def _silu(x):
    return x * jax.nn.sigmoid(x)


def _softplus(x):
    return jnp.maximum(x, 0.0) + jnp.log1p(jnp.exp(-jnp.abs(x)))


def _ada_kernel(c_ref, w_ref, b_ref, o_ref):
    cond = _silu(c_ref[...])
    o_ref[...] = _dot(cond, w_ref[...]) + b_ref[...]


def _ada(c, w_ada, b_ada):
    bsz, d = c.shape
    n = w_ada.shape[1]
    return pl.pallas_call(
        _ada_kernel,
        grid=(n // d,),
        in_specs=[
            pl.BlockSpec((bsz, d), lambda j: (0, 0)),
            pl.BlockSpec((d, d), lambda j: (0, j)),
            pl.BlockSpec((1, d), lambda j: (0, j)),
        ],
        out_specs=pl.BlockSpec((bsz, d), lambda j: (0, j)),
        out_shape=jax.ShapeDtypeStruct((bsz, n), F32),
        compiler_params=pltpu.CompilerParams(dimension_semantics=("arbitrary",)),
        name="ada",
    )(c, w_ada, b_ada.reshape(1, n))


def _ffn_kernel(x_ref, ada_ref, wg_ref, wu_ref, wd_ref, lng_ref, lnb_ref, o_ref, *, sub):
    x = x_ref[0]
    shift = ada_ref[0, 3 * sub:3 * sub + 1, :]
    scale = ada_ref[0, 3 * sub + 1:3 * sub + 2, :]
    gate = ada_ref[0, 3 * sub + 2:3 * sub + 3, :]
    h = (x * (1.0 + scale) + shift).astype(BF16)
    g = jnp.dot(h, wg_ref[...], preferred_element_type=F32)
    u = jnp.dot(h, wu_ref[...], preferred_element_type=F32)
    a = (_silu(g) * u).astype(BF16)
    f = jnp.dot(a, wd_ref[...], preferred_element_type=F32)
    y = DEEPNORM_ALPHA * x + (MACARON_WEIGHT * (1.0 + gate)) * f
    o_ref[0] = _layer_norm(y, lng_ref[...], lnb_ref[...])


def _const_spec(shape):
    nd = len(shape)
    return pl.BlockSpec(shape, lambda *_: (0,) * nd, pipeline_mode=pl.Buffered(1))


def _ffn(x, ada3, w_gate, w_up, w_down, ln_g, ln_b, sub):
    bsz, seq, d = x.shape
    dff = w_gate.shape[1]
    tm = FFN_TILE
    return pl.pallas_call(
        functools.partial(_ffn_kernel, sub=sub),
        grid=(bsz, seq // tm),
        in_specs=[
            pl.BlockSpec((1, tm, d), lambda b, i: (b, i, 0)),
            pl.BlockSpec((1, 3 * N_SUB, d), lambda b, i: (b, 0, 0)),
            _const_spec((d, dff)),
            _const_spec((d, dff)),
            _const_spec((dff, d)),
            _const_spec((1, d)),
            _const_spec((1, d)),
        ],
        out_specs=pl.BlockSpec((1, tm, d), lambda b, i: (b, i, 0)),
        out_shape=jax.ShapeDtypeStruct((bsz, seq, d), F32),
        compiler_params=pltpu.CompilerParams(
            dimension_semantics=("arbitrary", "arbitrary"), vmem_limit_bytes=VMEM_LIMIT_BYTES),
        name=f"ffn{sub}",
    )(x, ada3, w_gate.astype(BF16), w_up.astype(BF16), w_down.astype(BF16),
      ln_g.reshape(1, d), ln_b.reshape(1, d))


def _unit_lower_inverse(a_strict):
    t = a_strict.shape[0]
    row = lax.broadcasted_iota(jnp.int32, (t, t), 0)
    col = lax.broadcasted_iota(jnp.int32, (t, t), 1)
    eye = (row == col).astype(F32)
    same_block = (row // NEUMANN_BLOCK) == (col // NEUMANN_BLOCK)
    a_diag = jnp.where(same_block, a_strict, 0.0)
    a_off = a_strict - a_diag

    def neumann(m, index):
        acc = eye - m
        power = m
        span = 2
        while span < index:
            power = _dot(power, power)
            acc = acc + _dot(acc, power)
            span *= 2
        return acc

    t_diag = neumann(a_diag, NEUMANN_BLOCK)
    b = _dot(t_diag, a_off)
    q = neumann(b, t // NEUMANN_BLOCK)
    return _dot(q, t_diag)


def _mixer_kernel(x_ref, ada_ref, win_ref, sgu_g_ref, sgu_b_ref, ws_ref, bs_ref, convw_ref,
                  alog_ref, dtb_ref, alogc_ref, dtbc_ref, normw_ref, wout_ref, lng_ref, lnb_ref, o_ref,
                  state_ref, tail_ref):
    t = MIX_TILE
    d = x_ref.shape[-1]
    width = N_HEADS * HEAD_DIM

    @pl.when(pl.program_id(1) == 0)
    def _():
        state_ref[...] = jnp.zeros_like(state_ref)
        tail_ref[...] = jnp.zeros_like(tail_ref)

    x = x_ref[0]
    shift = ada_ref[0, 3:4, :]
    scale = ada_ref[0, 4:5, :]
    gate = ada_ref[0, 5:6, :]
    h = (x * (1.0 + scale) + shift).astype(BF16)
    proj = jnp.dot(h, win_ref[...], preferred_element_type=F32)

    o_u, o_v, o_qkv, o_z, o_gates, o_ab = 0, width, 2 * width, 5 * width, 6 * width, 8 * width

    row = lax.broadcasted_iota(jnp.int32, (t, t), 0)
    col = lax.broadcasted_iota(jnp.int32, (t, t), 1)

    u_act = jax.nn.gelu(proj[:, o_u:o_u + width])
    v_act = jax.nn.gelu(proj[:, o_v:o_v + width])
    v_ln = _layer_norm(v_act, sgu_g_ref[...], sgu_b_ref[...])
    chunk_mask = (row // SGU_CHUNK) >= (col // SGU_CHUNK)
    y_a = []
    for g in range(SGU_GROUPS):
        w = jnp.where(chunk_mask, ws_ref[g], 0.0)
        mixed = _dot(w, v_ln[:, g * LANES:(g + 1) * LANES]) + bs_ref[:, g:g + 1]
        y_a.append(u_act[:, g * LANES:(g + 1) * LANES] * mixed)

    qkv_pre = proj[:, o_qkv:o_qkv + 3 * width]
    tail = tail_ref[...]
    tail_ref[...] = qkv_pre[t - SUBLANES:, :]
    row8 = lax.broadcasted_iota(jnp.int32, (SUBLANES, 3 * width), 0)
    conv = qkv_pre * convw_ref[CONV_WIDTH - 1:CONV_WIDTH, :]
    for s in range(1, CONV_WIDTH):
        rolled = pltpu.roll(qkv_pre, s, 0)
        head = jnp.where(row8 < s, pltpu.roll(tail, s, 0), rolled[:SUBLANES, :])
        shifted = jnp.concatenate([head, rolled[SUBLANES:, :]], axis=0)
        conv = conv + shifted * convw_ref[CONV_WIDTH - 1 - s:CONV_WIDTH - s, :]
    qkv = _silu(conv)

    ab = proj[:, o_ab:o_ab + LANES]
    ab_t = ab.T
    g_full = -jnp.exp(alog_ref[...]) * _softplus(ab + dtb_ref[...])
    beta_full = jax.nn.sigmoid(ab)
    g_row = -jnp.exp(alogc_ref[...]) * _softplus(ab_t[0:N_HEADS, :] + dtbc_ref[...])
    lower = (row >= col).astype(F32)
    gc_col = jnp.dot(lower, g_full, preferred_element_type=F32, precision=lax.Precision.HIGHEST)
    gc_row = lax.dot_general(g_row, lower, (((1,), (1,)), ((), ())), preferred_element_type=F32,
                             precision=lax.Precision.HIGHEST)
    causal = row >= col
    strict = row > col

    y_b = []
    for hd in range(N_HEADS):
        lo = hd * HEAD_DIM
        q = qkv[:, lo:lo + HEAD_DIM]
        k = qkv[:, width + lo:width + lo + HEAD_DIM]
        v = qkv[:, 2 * width + lo:2 * width + lo + HEAD_DIM]
        q = q * lax.rsqrt(jnp.sum(q * q, axis=-1, keepdims=True) + L2_EPS) * (HEAD_DIM ** -0.5)
        k = k * lax.rsqrt(jnp.sum(k * k, axis=-1, keepdims=True) + L2_EPS)
        beta = beta_full[:, N_HEADS + hd:N_HEADS + hd + 1]
        gc_c = gc_col[:, hd:hd + 1]
        gc_r = gc_row[hd:hd + 1, :]
        gc_end = gc_r[:, t - 1:t]
        k_t = k.T
        decay = jnp.where(causal, jnp.exp(jnp.where(causal, gc_c - gc_r, 0.0)), 0.0)
        k_beta = k * beta
        a_mat = jnp.where(strict, _dot(k_beta, k_t) * decay, 0.0)
        t_inv = _unit_lower_inverse(a_mat)
        rhs = jnp.concatenate([v * beta, k_beta * jnp.exp(gc_c)], axis=1)
        uw = _dot(t_inv, rhs)
        u_val = uw[:, :HEAD_DIM]
        w_key = uw[:, HEAD_DIM:]
        qk = _dot(q, k_t) * decay
        state = state_ref[hd]
        v_new = u_val - _dot(w_key, state)
        out = _dot(q * jnp.exp(gc_c), state) + _dot(qk, v_new)
        state_ref[hd] = state * jnp.exp(gc_end) + _dot(k_t * jnp.exp(gc_end - gc_r), v_new)
        out = out * lax.rsqrt(jnp.mean(out * out, axis=-1, keepdims=True) + RMS_EPS) * normw_ref[...]
        y_b.append(out * _silu(proj[:, o_z + lo:o_z + lo + HEAD_DIM]))

    gates = jax.nn.sigmoid(proj[:, o_gates:o_gates + 2 * d])
    merged = gates[:, :d] * jnp.concatenate(y_a, axis=1) + gates[:, d:] * jnp.concatenate(y_b, axis=1)
    y = jnp.dot(merged.astype(BF16), wout_ref[...], preferred_element_type=F32)
    o_ref[0] = _layer_norm(DEEPNORM_ALPHA * x + (1.0 + gate) * y, lng_ref[...], lnb_ref[...])


def _mixer(x, ada3, w_in, sgu_ln_g, sgu_ln_b, sgu_w_s, sgu_b_s, conv_w, a_log, dt_bias, norm_w,
           w_out, ln_g, ln_b):
    bsz, seq, d = x.shape
    width = N_HEADS * HEAD_DIM
    u, v_s, qkv, z, a, b, gates = jnp.split(
        w_in, [width, 2 * width, 5 * width, 6 * width, 6 * width + N_HEADS, 6 * width + 2 * N_HEADS], axis=1)
    pad = jnp.zeros((d, LANES - 2 * N_HEADS), w_in.dtype)
    w_big = jnp.concatenate([u, v_s, qkv, z, gates, a, b, pad], axis=1).astype(BF16)
    ncols = w_big.shape[1]

    def lane_pad(p):
        return jnp.pad(p.reshape(1, N_HEADS), ((0, 0), (0, LANES - N_HEADS)))

    t = MIX_TILE
    return pl.pallas_call(
        _mixer_kernel,
        grid=(bsz, seq // t),
        in_specs=[
            pl.BlockSpec((1, t, d), lambda bi, i: (bi, i, 0)),
            pl.BlockSpec((1, 3 * N_SUB, d), lambda bi, i: (bi, 0, 0)),
            _const_spec((d, ncols)),
            _const_spec((1, width)),
            _const_spec((1, width)),
            _const_spec((SGU_GROUPS, SGU_BLOCK, SGU_BLOCK)),
            _const_spec((SGU_BLOCK, SGU_GROUPS)),
            _const_spec((CONV_WIDTH, 3 * width)),
            _const_spec((1, LANES)),
            _const_spec((1, LANES)),
            _const_spec((N_HEADS, 1)),
            _const_spec((N_HEADS, 1)),
            _const_spec((1, HEAD_DIM)),
            _const_spec((d, d)),
            _const_spec((1, d)),
            _const_spec((1, d)),
        ],
        out_specs=pl.BlockSpec((1, t, d), lambda bi, i: (bi, i, 0)),
        out_shape=jax.ShapeDtypeStruct((bsz, seq, d), F32),
        scratch_shapes=[
            pltpu.VMEM((N_HEADS, HEAD_DIM, HEAD_DIM), F32),
            pltpu.VMEM((SUBLANES, 3 * width), F32),
        ],
        compiler_params=pltpu.CompilerParams(
            dimension_semantics=("arbitrary", "arbitrary"), vmem_limit_bytes=VMEM_LIMIT_BYTES),
        name="mixer",
    )(x, ada3, w_big, sgu_ln_g.reshape(1, width), sgu_ln_b.reshape(1, width), sgu_w_s,
      sgu_b_s.T, conv_w, lane_pad(a_log), lane_pad(dt_bias), a_log.reshape(N_HEADS, 1),
      dt_bias.reshape(N_HEADS, 1),
      norm_w.reshape(1, HEAD_DIM), w_out.astype(BF16), ln_g.reshape(1, d), ln_b.reshape(1, d))


def kernel(x, c, w_ada, b_ada, ffn1_w_gate, ffn1_w_up, ffn1_w_down, ln1_g, ln1_b, mix_w_in, sgu_ln_g,
           sgu_ln_b, sgu_w_s, sgu_b_s, gdn_conv_w, gdn_a_log, gdn_dt_bias, gdn_norm_w, mix_w_out,
           ln2_g, ln2_b, ffn2_w_gate, ffn2_w_up, ffn2_w_down, ln3_g, ln3_b):
    bsz, _, d = x.shape
    for l in range(w_ada.shape[0]):
        ada3 = _ada(c, w_ada[l], b_ada[l]).reshape(bsz, 3 * N_SUB, d)
        x = _ffn(x, ada3, ffn1_w_gate[l], ffn1_w_up[l], ffn1_w_down[l], ln1_g[l], ln1_b[l], 0)
        x = _mixer(x, ada3, mix_w_in[l], sgu_ln_g[l], sgu_ln_b[l], sgu_w_s[l], sgu_b_s[l],
                   gdn_conv_w[l], gdn_a_log[l], gdn_dt_bias[l], gdn_norm_w[l], mix_w_out[l],
                   ln2_g[l], ln2_b[l])
        x = _ffn(x, ada3, ffn2_w_gate[l], ffn2_w_up[l], ffn2_w_down[l], ln3_g[l], ln3_b[l], 2)
    return x
```

```python
import functools

import jax
import jax.numpy as jnp
from jax import lax
from jax.experimental import pallas as pl
from jax.experimental.pallas import tpu as pltpu

F32 = jnp.float32
BF16 = jnp.bfloat16

N_SUB = 3
N_HEADS = 8
HEAD_DIM = 128
SGU_GROUPS = 8
SGU_BLOCK = 128
SGU_CHUNK = 64
CONV_WIDTH = 4
DEPTH = 1
MACARON_WEIGHT = 0.5
DEEPNORM_ALPHA = (2.0 * DEPTH) ** 0.25
LN_EPS = 1e-5
RMS_EPS = 1e-6
L2_EPS = 1e-6

LANES = 128
SUBLANES = 8
VMEM_LIMIT_BYTES = 56 * 1024 * 1024

MIX_TILE = 128
NEUMANN_BLOCK = 16
FFN_TILE = 512


def _dot(a, b):
    return jnp.dot(a.astype(BF16), b.astype(BF16), preferred_element_type=F32)


def _layer_norm(y, gain, bias):
    mu = jnp.mean(y, axis=-1, keepdims=True)
    yc = y - mu
    var = jnp.mean(yc * yc, axis=-1, keepdims=True)
    return yc * lax.rsqrt(var + LN_EPS) * gain + bias


def _silu(x):
    return x * jax.nn.sigmoid(x)


def _softplus(x):
    return jnp.maximum(x, 0.0) + jnp.log1p(jnp.exp(-jnp.abs(x)))


def _const_spec(shape):
    nd = len(shape)
    return pl.BlockSpec(shape, lambda *_: (0,) * nd, pipeline_mode=pl.Buffered(1))


def _ada_kernel(c_ref, w_ref, b_ref, o_ref):
    cond = _silu(c_ref[...])
    o_ref[...] = _dot(cond, w_ref[...]) + b_ref[...]


def _ada(c, w_ada, b_ada):
    bsz, d = c.shape
    n = w_ada.shape[1]
    return pl.pallas_call(
        _ada_kernel,
        grid=(n // d,),
        in_specs=[
            pl.BlockSpec((bsz, d), lambda j: (0, 0)),
            pl.BlockSpec((d, d), lambda j: (0, j)),
            pl.BlockSpec((1, d), lambda j: (0, j)),
        ],
        out_specs=pl.BlockSpec((bsz, d), lambda j: (0, j)),
        out_shape=jax.ShapeDtypeStruct((bsz, n), F32),
        compiler_params=pltpu.CompilerParams(dimension_semantics=("arbitrary",)),
        name="ada",
    )(c, w_ada, b_ada.reshape(1, n))


def _ffn_kernel(x_ref, ada_ref, wg_ref, wu_ref, wd_ref, lng_ref, lnb_ref, o_ref, *, sub):
    x = x_ref[0]
    shift = ada_ref[0, 3 * sub:3 * sub + 1, :]
    scale = ada_ref[0, 3 * sub + 1:3 * sub + 2, :]
    gate = ada_ref[0, 3 * sub + 2:3 * sub + 3, :]
    h = (x * (1.0 + scale) + shift).astype(BF16)
    g = jnp.dot(h, wg_ref[...], preferred_element_type=F32)
    u = jnp.dot(h, wu_ref[...], preferred_element_type=F32)
    a = (_silu(g) * u).astype(BF16)
    f = jnp.dot(a, wd_ref[...], preferred_element_type=F32)
    y = DEEPNORM_ALPHA * x + (MACARON_WEIGHT * (1.0 + gate)) * f
    o_ref[0] = _layer_norm(y, lng_ref[...], lnb_ref[...])


def _ffn(x, ada3, w_gate, w_up, w_down, ln_g, ln_b, sub):
    bsz, seq, d = x.shape
    dff = w_gate.shape[1]
    tm = FFN_TILE
    return pl.pallas_call(
        functools.partial(_ffn_kernel, sub=sub),
        grid=(bsz, seq // tm),
        in_specs=[
            pl.BlockSpec((1, tm, d), lambda b, i: (b, i, 0)),
            pl.BlockSpec((1, 3 * N_SUB, d), lambda b, i: (b, 0, 0)),
            _const_spec((d, dff)),
            _const_spec((d, dff)),
            _const_spec((dff, d)),
            _const_spec((1, d)),
            _const_spec((1, d)),
        ],
        out_specs=pl.BlockSpec((1, tm, d), lambda b, i: (b, i, 0)),
        out_shape=jax.ShapeDtypeStruct((bsz, seq, d), F32),
        compiler_params=pltpu.CompilerParams(
            dimension_semantics=("arbitrary", "arbitrary"), vmem_limit_bytes=VMEM_LIMIT_BYTES),
        name=f"ffn{sub}",
    )(x, ada3, w_gate.astype(BF16), w_up.astype(BF16), w_down.astype(BF16),
      ln_g.reshape(1, d), ln_b.reshape(1, d))


def _unit_lower_inverse(a_list):
    t = a_list[0].shape[0]
    row = lax.broadcasted_iota(jnp.int32, (t, t), 0)
    col = lax.broadcasted_iota(jnp.int32, (t, t), 1)
    eye = (row == col).astype(F32)
    same_block = (row // NEUMANN_BLOCK) == (col // NEUMANN_BLOCK)
    a_diag = [jnp.where(same_block, a, 0.0) for a in a_list]
    a_off = [a - ad for a, ad in zip(a_list, a_diag)]

    def neumann(ms, index):
        acc = [eye - m for m in ms]
        power = ms
        span = 2
        while span < index:
            power = [_dot(p, p) for p in power]
            acc = [a + _dot(a, p) for a, p in zip(acc, power)]
            span *= 2
        return acc

    t_diag = neumann(a_diag, NEUMANN_BLOCK)
    b = [_dot(td, ao) for td, ao in zip(t_diag, a_off)]
    q = neumann(b, t // NEUMANN_BLOCK)
    return [_dot(qi, td) for qi, td in zip(q, t_diag)]


def _causal_conv_silu(pre, tail, conv_w):
    rows = lax.broadcasted_iota(jnp.int32, tail.shape, 0)
    conv = pre * conv_w[CONV_WIDTH - 1:CONV_WIDTH, :]
    for s in range(1, CONV_WIDTH):
        rolled = pltpu.roll(pre, s, 0)
        head = jnp.where(rows < s, pltpu.roll(tail, s, 0), rolled[:SUBLANES, :])
        shifted = jnp.concatenate([head, rolled[SUBLANES:, :]], axis=0)
        conv = conv + shifted * conv_w[CONV_WIDTH - 1 - s:CONV_WIDTH - s, :]
    return _silu(conv)


def _mixer_kernel(x_ref, ada_ref, win_ref, wgates_ref, sgu_g_ref, sgu_b_ref, ws_ref, bs_ref, convw_ref,
                  alog_ref, dtb_ref, alogc_ref, dtbc_ref, normw_ref, wout_ref, lng_ref, lnb_ref, o_ref,
                  state_ref, tail_ref):
    nb, t, d = x_ref.shape
    width = N_HEADS * HEAD_DIM
    batches = range(nb)
    units = [(b, hd) for b in batches for hd in range(N_HEADS)]

    @pl.when(pl.program_id(0) == 0)
    def _():
        state_ref[...] = jnp.zeros_like(state_ref)
        tail_ref[...] = jnp.zeros_like(tail_ref)

    xs = [x_ref[b] for b in batches]
    h = jnp.concatenate(
        [(xs[b] * (1.0 + ada_ref[b, 4:5, :]) + ada_ref[b, 3:4, :]).astype(BF16) for b in batches], axis=0)
    proj_main = jnp.dot(h, win_ref[...], preferred_element_type=F32)
    proj_gates = jnp.dot(h, wgates_ref[...], preferred_element_type=F32)
    proj_ab = proj_gates[:, 2 * d:]
    proj =[proj_main[b * t:(b + 1) * t, :] for b in batches]

    o_u, o_v, o_qkv, o_z = 0, width, 2 * width, 5 * width

    row = lax.broadcasted_iota(jnp.int32, (t, t), 0)
    col = lax.broadcasted_iota(jnp.int32, (t, t), 1)
    causal = row >= col
    strict = row > col
    lower = causal.astype(F32)

    u_act = [jax.nn.gelu(p[:, o_u:o_u + width]) for p in proj]
    v_ln = [_layer_norm(jax.nn.gelu(p[:, o_v:o_v + width]), sgu_g_ref[...], sgu_b_ref[...]) for p in proj]
    chunk_mask = (row // SGU_CHUNK) >= (col // SGU_CHUNK)
    y_a = [[] for _ in batches]
    for g in range(SGU_GROUPS):
        w = jnp.where(chunk_mask, ws_ref[g], 0.0)
        sl = slice(g * LANES, (g + 1) * LANES)
        mixed = _dot(w, jnp.concatenate([v_ln[b][:, sl] for b in batches], axis=1))
        for b in batches:
            y_a[b].append(u_act[b][:, sl] * (mixed[:, b * LANES:(b + 1) * LANES] + bs_ref[:, g:g + 1]))

    qkv, beta_full, gc_col, gc_row = [], [], [], []
    for b in batches:
        pre = proj[b][:, o_qkv:o_qkv + 3 * width]
        tail = tail_ref[b]
        tail_ref[b] = pre[t - SUBLANES:, :]
        qkv.append(_causal_conv_silu(pre, tail, convw_ref[...]))
        ab = proj_ab[b * t:(b + 1) * t, :]
        ab_t = ab.T
        g_col = -jnp.exp(alog_ref[...]) * _softplus(ab + dtb_ref[...])
        g_row = -jnp.exp(alogc_ref[...]) * _softplus(ab_t[0:N_HEADS, :] + dtbc_ref[...])
        beta_full.append(jax.nn.sigmoid(ab))
        gc_col.append(jnp.dot(lower, g_col, preferred_element_type=F32, precision=lax.Precision.HIGHEST))
        gc_row.append(lax.dot_general(g_row, lower, (((1,), (1,)), ((), ())),
                                      preferred_element_type=F32, precision=lax.Precision.HIGHEST))

    q, k_t, v_beta, k_beta, decay, gc_c, gc_r = [], [], [], [], [], [], []
    for b, hd in units:
        lo = hd * HEAD_DIM
        qh = qkv[b][:, lo:lo + HEAD_DIM]
        kh = qkv[b][:, width + lo:width + lo + HEAD_DIM]
        vh = qkv[b][:, 2 * width + lo:2 * width + lo + HEAD_DIM]
        qh = qh * lax.rsqrt(jnp.sum(qh * qh, axis=-1, keepdims=True) + L2_EPS) * (HEAD_DIM ** -0.5)
        kh = kh * lax.rsqrt(jnp.sum(kh * kh, axis=-1, keepdims=True) + L2_EPS)
        beta = beta_full[b][:, N_HEADS + hd:N_HEADS + hd + 1]
        c = gc_col[b][:, hd:hd + 1]
        r = gc_row[b][hd:hd + 1, :]
        q.append(qh)
        k_t.append(kh.T)
        v_beta.append(vh * beta)
        k_beta.append(kh * beta)
        decay.append(jnp.where(causal, jnp.exp(jnp.where(causal, c - r, 0.0)), 0.0))
        gc_c.append(c)
        gc_r.append(r)

    n = range(len(units))
    a_mat = [jnp.where(strict, _dot(k_beta[i], k_t[i]) * decay[i], 0.0) for i in n]
    qk = [_dot(q[i], k_t[i]) * decay[i] for i in n]
    t_inv = _unit_lower_inverse(a_mat)
    uw = [_dot(t_inv[i], jnp.concatenate([v_beta[i], k_beta[i] * jnp.exp(gc_c[i])], axis=1)) for i in n]
    state = [state_ref[b, hd] for b, hd in units]
    v_new = [uw[i][:, :HEAD_DIM] - _dot(uw[i][:, HEAD_DIM:], state[i]) for i in n]
    out = [_dot(q[i] * jnp.exp(gc_c[i]), state[i]) + _dot(qk[i], v_new[i]) for i in n]
    for i, (b, hd) in enumerate(units):
        gc_end = gc_r[i][:, t - 1:t]
        state_ref[b, hd] = state[i] * jnp.exp(gc_end) + _dot(k_t[i] * jnp.exp(gc_end - gc_r[i]), v_new[i])
    y_b = [[] for _ in batches]
    for i, (b, hd) in enumerate(units):
        lo = o_z + hd * HEAD_DIM
        o = out[i]
        o = o * lax.rsqrt(jnp.mean(o * o, axis=-1, keepdims=True) + RMS_EPS) * normw_ref[...]
        y_b[b].append(o * _silu(proj[b][:, lo:lo + HEAD_DIM]))

    merged = []
    for b in batches:
        gates = jax.nn.sigmoid(proj_gates[b * t:(b + 1) * t, :2 * d])
        merged.append((gates[:, :d] * jnp.concatenate(y_a[b], axis=1)
                       + gates[:, d:] * jnp.concatenate(y_b[b], axis=1)).astype(BF16))
    y = jnp.dot(jnp.concatenate(merged, axis=0), wout_ref[...], preferred_element_type=F32)
    for b in batches:
        res = DEEPNORM_ALPHA * xs[b] + (1.0 + ada_ref[b, 5:6, :]) * y[b * t:(b + 1) * t, :]
        o_ref[b] = _layer_norm(res, lng_ref[...], lnb_ref[...])


def _mixer(x, ada3, w_in, sgu_ln_g, sgu_ln_b, sgu_w_s, sgu_b_s, conv_w, a_log, dt_bias, norm_w,
           w_out, ln_g, ln_b):
    bsz, seq, d = x.shape
    width = N_HEADS * HEAD_DIM
    n_main = 6 * width
    w_main = w_in[:, :n_main].astype(BF16)
    w_ab = jnp.pad(w_in[:, n_main:n_main + 2 * N_HEADS], ((0, 0), (0, LANES - 2 * N_HEADS)))
    w_gates = jnp.concatenate([w_in[:, n_main + 2 * N_HEADS:], w_ab], axis=1).astype(BF16)

    def lane_pad(p):
        return jnp.pad(p.reshape(1, N_HEADS), ((0, 0), (0, LANES - N_HEADS)))

    t = MIX_TILE
    return pl.pallas_call(
        _mixer_kernel,
        grid=(seq // t,),
        in_specs=[
            pl.BlockSpec((bsz, t, d), lambda i: (0, i, 0)),
            _const_spec((bsz, 3 * N_SUB, d)),
            _const_spec((d, n_main)),
            _const_spec((d, 2 * d + LANES)),
            _const_spec((1, width)),
            _const_spec((1, width)),
            _const_spec((SGU_GROUPS, SGU_BLOCK, SGU_BLOCK)),
            _const_spec((SGU_BLOCK, SGU_GROUPS)),
            _const_spec((CONV_WIDTH, 3 * width)),
            _const_spec((1, LANES)),
            _const_spec((1, LANES)),
            _const_spec((N_HEADS, 1)),
            _const_spec((N_HEADS, 1)),
            _const_spec((1, HEAD_DIM)),
            _const_spec((d, d)),
            _const_spec((1, d)),
            _const_spec((1, d)),
        ],
        out_specs=pl.BlockSpec((bsz, t, d), lambda i: (0, i, 0)),
        out_shape=jax.ShapeDtypeStruct((bsz, seq, d), F32),
        scratch_shapes=[
            pltpu.VMEM((bsz, N_HEADS, HEAD_DIM, HEAD_DIM), F32),
            pltpu.VMEM((bsz, SUBLANES, 3 * width), F32),
        ],
        compiler_params=pltpu.CompilerParams(
            dimension_semantics=("arbitrary",), vmem_limit_bytes=VMEM_LIMIT_BYTES),
        name="mixer",
    )(x, ada3, w_main, w_gates, sgu_ln_g.reshape(1, width), sgu_ln_b.reshape(1, width), sgu_w_s,
      sgu_b_s.T, conv_w, lane_pad(a_log), lane_pad(dt_bias), a_log.reshape(N_HEADS, 1),
      dt_bias.reshape(N_HEADS, 1), norm_w.reshape(1, HEAD_DIM), w_out.astype(BF16),
      ln_g.reshape(1, d), ln_b.reshape(1, d))


def kernel(x, c, w_ada, b_ada, ffn1_w_gate, ffn1_w_up, ffn1_w_down, ln1_g, ln1_b, mix_w_in, sgu_ln_g,
           sgu_ln_b, sgu_w_s, sgu_b_s, gdn_conv_w, gdn_a_log, gdn_dt_bias, gdn_norm_w, mix_w_out,
           ln2_g, ln2_b, ffn2_w_gate, ffn2_w_up, ffn2_w_down, ln3_g, ln3_b):
    bsz, _, d = x.shape
    for l in range(w_ada.shape[0]):
        ada3 = _ada(c, w_ada[l], b_ada[l]).reshape(bsz, 3 * N_SUB, d)
        x = _ffn(x, ada3, ffn1_w_gate[l], ffn1_w_up[l], ffn1_w_down[l], ln1_g[l], ln1_b[l], 0)
        x = _mixer(x, ada3, mix_w_in[l], sgu_ln_g[l], sgu_ln_b[l], sgu_w_s[l], sgu_b_s[l],
                   gdn_conv_w[l], gdn_a_log[l], gdn_dt_bias[l], gdn_norm_w[l], mix_w_out[l],
                   ln2_g[l], ln2_b[l])
        x = _ffn(x, ada3, ffn2_w_gate[l], ffn2_w_up[l], ffn2_w_down[l], ln3_g[l], ln3_b[l], 2)
    return x
```

```python
import functools

import jax
import jax.numpy as jnp
from jax import lax
from jax.experimental import pallas as pl
from jax.experimental.pallas import tpu as pltpu

F32 = jnp.float32
BF16 = jnp.bfloat16

N_SUB = 3
N_HEADS = 8
HEAD_DIM = 128
SGU_GROUPS = 8
SGU_BLOCK = 128
SGU_CHUNK = 64
CONV_WIDTH = 4
DEPTH = 1
MACARON_WEIGHT = 0.5
DEEPNORM_ALPHA = (2.0 * DEPTH) ** 0.25
LN_EPS = 1e-5
RMS_EPS = 1e-6
L2_EPS = 1e-6

LANES = 128
SUBLANES = 8
VMEM_LIMIT_BYTES = 56 * 1024 * 1024

MIX_TILE = 128
NEUMANN_BLOCK = 16
FFN_TILE = 1024
FFN_PARTS = 4


def _dot(a, b):
    return jnp.dot(a.astype(BF16), b.astype(BF16), preferred_element_type=F32)


def _layer_norm(y, gain, bias):
    mu = jnp.mean(y, axis=-1, keepdims=True)
    yc = y - mu
    var = jnp.mean(yc * yc, axis=-1, keepdims=True)
    return yc * lax.rsqrt(var + LN_EPS) * gain + bias


def _silu(x):
    return x * jax.nn.sigmoid(x)


def _softplus(x):
    return jnp.maximum(x, 0.0) + jnp.log1p(jnp.exp(-jnp.abs(x)))


def _const_spec(shape):
    nd = len(shape)
    return pl.BlockSpec(shape, lambda *_: (0,) * nd, pipeline_mode=pl.Buffered(1))


def _ada_kernel(c_ref, w_ref, b_ref, o_ref):
    cond = _silu(c_ref[...])
    o_ref[...] = _dot(cond, w_ref[0]) + b_ref[...]


def _ada(c, w_ada, b_ada, layer):
    bsz, d = c.shape
    n = w_ada.shape[2]
    return pl.pallas_call(
        _ada_kernel,
        grid=(n // d,),
        in_specs=[
            pl.BlockSpec((bsz, d), lambda j: (0, 0)),
            pl.BlockSpec((1, d, d), lambda j: (layer, 0, j)),
            pl.BlockSpec((1, d), lambda j: (0, j)),
        ],
        out_specs=pl.BlockSpec((bsz, d), lambda j: (0, j)),
        out_shape=jax.ShapeDtypeStruct((bsz, n), F32),
        compiler_params=pltpu.CompilerParams(dimension_semantics=("arbitrary",)),
        name="ada",
    )(c, w_ada, b_ada.reshape(1, n))


def _ffn_kernel(x_ref, ada_ref, wg_ref, wu_ref, wd_ref, lng_ref, lnb_ref, o_ref, *, sub):
    shift = ada_ref[0, 3 * sub:3 * sub + 1, :]
    scale = ada_ref[0, 3 * sub + 1:3 * sub + 2, :]
    gate = ada_ref[0, 3 * sub + 2:3 * sub + 3, :]
    tm = x_ref.shape[1]
    rows = [slice(i * tm // FFN_PARTS, (i + 1) * tm // FFN_PARTS) for i in range(FFN_PARTS)]
    xs = [x_ref[0, r, :] for r in rows]
    acts = []
    for x in xs:
        h = (x * (1.0 + scale) + shift).astype(BF16)
        g = jnp.dot(h, wg_ref[...], preferred_element_type=F32)
        u = jnp.dot(h, wu_ref[...], preferred_element_type=F32)
        acts.append((_silu(g) * u).astype(BF16))
    for r, x, a in zip(rows, xs, acts):
        f = jnp.dot(a, wd_ref[...], preferred_element_type=F32)
        y = DEEPNORM_ALPHA * x + (MACARON_WEIGHT * (1.0 + gate)) * f
        o_ref[0, r, :] = _layer_norm(y, lng_ref[...], lnb_ref[...])


def _ffn(x, ada3, w_gate, w_up, w_down, ln_g, ln_b, sub):
    bsz, seq, d = x.shape
    dff = w_gate.shape[1]
    tm = FFN_TILE
    return pl.pallas_call(
        functools.partial(_ffn_kernel, sub=sub),
        grid=(bsz, seq // tm),
        in_specs=[
            pl.BlockSpec((1, tm, d), lambda b, i: (b, i, 0)),
            pl.BlockSpec((1, 3 * N_SUB, d), lambda b, i: (b, 0, 0)),
            _const_spec((d, dff)),
            _const_spec((d, dff)),
            _const_spec((dff, d)),
            _const_spec((1, d)),
            _const_spec((1, d)),
        ],
        out_specs=pl.BlockSpec((1, tm, d), lambda b, i: (b, i, 0)),
        out_shape=jax.ShapeDtypeStruct((bsz, seq, d), F32),
        compiler_params=pltpu.CompilerParams(
            dimension_semantics=("arbitrary", "arbitrary"), vmem_limit_bytes=VMEM_LIMIT_BYTES),
        name=f"ffn{sub}",
    )(x, ada3, w_gate.astype(BF16), w_up.astype(BF16), w_down.astype(BF16),
      ln_g.reshape(1, d), ln_b.reshape(1, d))


def _block_diag(pair):
    t = pair.shape[0]
    pb = pair.astype(BF16)
    zero = jnp.zeros((t, t), BF16)
    return jnp.concatenate([jnp.concatenate([pb[:, :t], zero], axis=1),
                            jnp.concatenate([zero, pb[:, t:]], axis=1)], axis=0)


def _pair_dot(x, y):
    return jnp.dot(x.astype(BF16), _block_diag(y), preferred_element_type=F32)


def _unit_lower_inverse(a_list, row, col):
    t = a_list[0].shape[0]
    eye = (row == col).astype(F32)
    same_block = (row // NEUMANN_BLOCK) == (col // NEUMANN_BLOCK)
    a_diag = [jnp.where(same_block, a, 0.0) for a in a_list]
    a_off = [a - ad for a, ad in zip(a_list, a_diag)]

    def neumann(ms, index):
        acc = [eye - m for m in ms]
        power = ms
        span = 2
        while span < index:
            power = [_pair_dot(p, p) for p in power]
            acc = [a + _pair_dot(a, p) for a, p in zip(acc, power)]
            span *= 2
        return acc

    t_diag = neumann(a_diag, NEUMANN_BLOCK)
    b = [_pair_dot(td, ao) for td, ao in zip(t_diag, a_off)]
    q = neumann(b, t // NEUMANN_BLOCK)
    return [_pair_dot(qi, td) for qi, td in zip(q, t_diag)]


def _causal_conv_silu(pre, tail, conv_w):
    rows = lax.broadcasted_iota(jnp.int32, tail.shape, 0)
    conv = pre * conv_w[CONV_WIDTH - 1:CONV_WIDTH, :]
    for s in range(1, CONV_WIDTH):
        rolled = pltpu.roll(pre, s, 0)
        head = jnp.where(rows < s, pltpu.roll(tail, s, 0), rolled[:SUBLANES, :])
        shifted = jnp.concatenate([head, rolled[SUBLANES:, :]], axis=0)
        conv = conv + shifted * conv_w[CONV_WIDTH - 1 - s:CONV_WIDTH - s, :]
    return _silu(conv)


def _mixer_kernel(x_ref, ada_ref, win_ref, wgates_ref, sgu_g_ref, sgu_b_ref, ws_ref, bs_ref, convw_ref,
                  alog_ref, dtb_ref, alogc_ref, dtbc_ref, normw_ref, wout_ref, lng_ref, lnb_ref, o_ref,
                  state_ref, tail_ref):
    nb, t, d = x_ref.shape
    width = N_HEADS * HEAD_DIM
    batches = range(nb)

    @pl.when(pl.program_id(0) == 0)
    def _():
        state_ref[...] = jnp.zeros_like(state_ref)
        tail_ref[...] = jnp.zeros_like(tail_ref)

    xs = [x_ref[b] for b in batches]
    h = jnp.concatenate(
        [(xs[b] * (1.0 + ada_ref[b, 4:5, :]) + ada_ref[b, 3:4, :]).astype(BF16) for b in batches], axis=0)
    proj_main = jnp.dot(h, win_ref[...], preferred_element_type=F32)
    proj_gates = jnp.dot(h, wgates_ref[...], preferred_element_type=F32)
    proj = [proj_main[b * t:(b + 1) * t, :] for b in batches]

    o_u, o_v, o_qkv, o_z, o_ab = 0, width, 2 * width, 5 * width, 6 * width

    row = lax.broadcasted_iota(jnp.int32, (t, t), 0)
    col = lax.broadcasted_iota(jnp.int32, (t, t), 1)
    lower = (row >= col).astype(F32)

    u_act = [jax.nn.gelu(p[:, o_u:o_u + width]) for p in proj]
    v_ln = [_layer_norm(jax.nn.gelu(p[:, o_v:o_v + width]), sgu_g_ref[...], sgu_b_ref[...]) for p in proj]
    chunk_mask = (row // SGU_CHUNK) >= (col // SGU_CHUNK)
    y_a = [[] for _ in batches]
    for g in range(SGU_GROUPS):
        w = jnp.where(chunk_mask, ws_ref[g], 0.0)
        sl = slice(g * LANES, (g + 1) * LANES)
        mixed = _dot(w, jnp.concatenate([v_ln[b][:, sl] for b in batches], axis=1))
        for b in batches:
            y_a[b].append(u_act[b][:, sl] * (mixed[:, b * LANES:(b + 1) * LANES] + bs_ref[:, g:g + 1]))

    qkv, beta_full, gc_col, gc_row = [], [], [], []
    for b in batches:
        pre = proj[b][:, o_qkv:o_qkv + 3 * width]
        tail = tail_ref[b]
        tail_ref[b] = pre[t - SUBLANES:, :]
        qkv.append(_causal_conv_silu(pre, tail, convw_ref[...]))
        ab = proj[b][:, o_ab:o_ab + LANES]
        ab_t = ab.T
        g_col = -jnp.exp(alog_ref[...]) * _softplus(ab + dtb_ref[...])
        g_row = -jnp.exp(alogc_ref[...]) * _softplus(ab_t[0:N_HEADS, :] + dtbc_ref[...])
        beta_full.append(jax.nn.sigmoid(ab))
        gc_col.append(jnp.dot(lower, g_col, preferred_element_type=F32, precision=lax.Precision.HIGHEST))
        gc_row.append(lax.dot_general(g_row, lower, (((1,), (1,)), ((), ())),
                                      preferred_element_type=F32, precision=lax.Precision.HIGHEST))

    hd2 = 2 * HEAD_DIM
    row2 = lax.broadcasted_iota(jnp.int32, (t, 2 * t), 0)
    col2 = lax.broadcasted_iota(jnp.int32, (t, 2 * t), 1) % t
    causal2 = row2 >= col2
    strict2 = row2 > col2
    pairs = [(b, j) for b in batches for j in range(N_HEADS // 2)]

    def per_half(f):
        return jnp.concatenate([f(0), f(1)], axis=1)

    def cols(x, first):
        return per_half(lambda i: jnp.broadcast_to(x[:, first + i:first + i + 1], (t, HEAD_DIM)))

    def l2_normalize(xp, scale):
        return xp * per_half(lambda i: jnp.broadcast_to(scale * lax.rsqrt(jnp.sum(
            jnp.square(xp[:, i * HEAD_DIM:(i + 1) * HEAD_DIM]), axis=-1, keepdims=True) + L2_EPS), (t, HEAD_DIM)))

    exp_gc_col = [jnp.exp(g) for g in gc_col]

    q, q_decayed, k_t, v_beta, k_beta, k_beta_g, decay, k_t_end, end_scale = [], [], [], [], [], [], [], [], []
    for b, j in pairs:
        hd = 2 * j
        lo = hd * HEAD_DIM
        qp = l2_normalize(qkv[b][:, lo:lo + hd2], HEAD_DIM ** -0.5)
        kp = l2_normalize(qkv[b][:, width + lo:width + lo + hd2], 1.0)
        vp = qkv[b][:, 2 * width + lo:2 * width + lo + hd2]
        beta = cols(beta_full[b], N_HEADS + hd)
        c = cols(gc_col[b], hd)
        r = per_half(lambda i: gc_row[b][hd + i:hd + i + 1, :])
        end = per_half(lambda i: jnp.broadcast_to(gc_row[b][hd + i:hd + i + 1, t - 1:t], (1, t)))
        ktp = per_half(lambda i: kp[:, i * HEAD_DIM:(i + 1) * HEAD_DIM].T)
        e_c = cols(exp_gc_col[b], hd)
        q.append(qp)
        q_decayed.append(qp * e_c)
        k_t.append(ktp)
        v_beta.append(vp * beta)
        k_beta.append(kp * beta)
        k_beta_g.append(kp * beta * e_c)
        decay.append(jnp.exp(jnp.where(causal2, c - r, -jnp.inf)))
        k_t_end.append(ktp * jnp.exp(end - r))
        end_scale.append(jnp.exp(end))

    n = range(len(pairs))
    a_mat = [jnp.where(strict2, _pair_dot(k_beta[p], k_t[p]) * decay[p], 0.0) for p in n]
    qk = [_pair_dot(q[p], k_t[p]) * decay[p] for p in n]
    t_inv = _unit_lower_inverse(a_mat, row2, col2)
    uw = [[_dot(t_inv[p][:, i * t:(i + 1) * t],
                jnp.concatenate([v_beta[p][:, i * HEAD_DIM:(i + 1) * HEAD_DIM],
                                 k_beta_g[p][:, i * HEAD_DIM:(i + 1) * HEAD_DIM]], axis=1))
           for i in range(2)] for p in n]
    u_val = [per_half(lambda i: uw[p][i][:, :HEAD_DIM]) for p in n]
    w_key = [per_half(lambda i: uw[p][i][:, HEAD_DIM:]) for p in n]
    state = [state_ref[p] for p in n]
    from_state = [jnp.dot(jnp.concatenate([w_key[p], q_decayed[p]], axis=0).astype(BF16),
                          _block_diag(state[p]), preferred_element_type=F32) for p in n]
    v_new = [u_val[p] - from_state[p][:t] for p in n]
    out = [from_state[p][t:] + _pair_dot(qk[p], v_new[p]) for p in n]
    for p in n:
        state_ref[p] = state[p] * end_scale[p] + _pair_dot(k_t_end[p], v_new[p])
    y_b = [[] for _ in batches]
    norm_w2 = per_half(lambda i: normw_ref[...])
    for p, (b, j) in enumerate(pairs):
        lo = o_z + 2 * j * HEAD_DIM
        o = out[p]
        o = o * per_half(lambda i: jnp.broadcast_to(lax.rsqrt(jnp.mean(
            jnp.square(o[:, i * HEAD_DIM:(i + 1) * HEAD_DIM]), axis=-1, keepdims=True) + RMS_EPS), (t, HEAD_DIM)))
        y_b[b].append(o * norm_w2 * _silu(proj[b][:, lo:lo + hd2]))

    merged = []
    for b in batches:
        gates = jax.nn.sigmoid(proj_gates[b * t:(b + 1) * t, :])
        merged.append((gates[:, :d] * jnp.concatenate(y_a[b], axis=1)
                       + gates[:, d:] * jnp.concatenate(y_b[b], axis=1)).astype(BF16))
    y = jnp.dot(jnp.concatenate(merged, axis=0), wout_ref[...], preferred_element_type=F32)
    for b in batches:
        res = DEEPNORM_ALPHA * xs[b] + (1.0 + ada_ref[b, 5:6, :]) * y[b * t:(b + 1) * t, :]
        o_ref[b] = _layer_norm(res, lng_ref[...], lnb_ref[...])


def _mixer(x, ada3, w_in, sgu_ln_g, sgu_ln_b, sgu_w_s, sgu_b_s, conv_w, a_log, dt_bias, norm_w,
           w_out, ln_g, ln_b):
    bsz, seq, d = x.shape
    width = N_HEADS * HEAD_DIM
    assert MIX_TILE == HEAD_DIM and N_HEADS % 2 == 0
    n_main = 6 * width + LANES
    n_ab = 6 * width + 2 * N_HEADS
    w_main = jnp.pad(w_in[:, :n_ab], ((0, 0), (0, n_main - n_ab))).astype(BF16)
    w_gates = w_in[:, n_ab:].astype(BF16)

    def lane_pad(p):
        return jnp.pad(p.reshape(1, N_HEADS), ((0, 0), (0, LANES - N_HEADS)))

    t = MIX_TILE
    return pl.pallas_call(
        _mixer_kernel,
        grid=(seq // t,),
        in_specs=[
            pl.BlockSpec((bsz, t, d), lambda i: (0, i, 0)),
            _const_spec((bsz, 3 * N_SUB, d)),
            _const_spec((d, n_main)),
            _const_spec((d, 2 * d)),
            _const_spec((1, width)),
            _const_spec((1, width)),
            _const_spec((SGU_GROUPS, SGU_BLOCK, SGU_BLOCK)),
            _const_spec((SGU_BLOCK, SGU_GROUPS)),
            _const_spec((CONV_WIDTH, 3 * width)),
            _const_spec((1, LANES)),
            _const_spec((1, LANES)),
            _const_spec((N_HEADS, 1)),
            _const_spec((N_HEADS, 1)),
            _const_spec((1, HEAD_DIM)),
            _const_spec((d, d)),
            _const_spec((1, d)),
            _const_spec((1, d)),
        ],
        out_specs=pl.BlockSpec((bsz, t, d), lambda i: (0, i, 0)),
        out_shape=jax.ShapeDtypeStruct((bsz, seq, d), F32),
        scratch_shapes=[
            pltpu.VMEM((bsz * N_HEADS // 2, HEAD_DIM, 2 * HEAD_DIM), F32),
            pltpu.VMEM((bsz, SUBLANES, 3 * width), F32),
        ],
        compiler_params=pltpu.CompilerParams(
            dimension_semantics=("arbitrary",), vmem_limit_bytes=VMEM_LIMIT_BYTES),
        name="mixer",
    )(x, ada3, w_main, w_gates, sgu_ln_g.reshape(1, width), sgu_ln_b.reshape(1, width), sgu_w_s,
      sgu_b_s.T, conv_w, lane_pad(a_log), lane_pad(dt_bias), a_log.reshape(N_HEADS, 1),
      dt_bias.reshape(N_HEADS, 1), norm_w.reshape(1, HEAD_DIM), w_out.astype(BF16),
      ln_g.reshape(1, d), ln_b.reshape(1, d))


def kernel(x, c, w_ada, b_ada, ffn1_w_gate, ffn1_w_up, ffn1_w_down, ln1_g, ln1_b, mix_w_in, sgu_ln_g,
           sgu_ln_b, sgu_w_s, sgu_b_s, gdn_conv_w, gdn_a_log, gdn_dt_bias, gdn_norm_w, mix_w_out,
           ln2_g, ln2_b, ffn2_w_gate, ffn2_w_up, ffn2_w_down, ln3_g, ln3_b):
    bsz, _, d = x.shape
    for l in range(w_ada.shape[0]):
        ada3 = _ada(c, w_ada, b_ada[l], l).reshape(bsz, 3 * N_SUB, d)
        x = _ffn(x, ada3, ffn1_w_gate[l], ffn1_w_up[l], ffn1_w_down[l], ln1_g[l], ln1_b[l], 0)
        x = _mixer(x, ada3, mix_w_in[l], sgu_ln_g[l], sgu_ln_b[l], sgu_w_s[l], sgu_b_s[l],
                   gdn_conv_w[l], gdn_a_log[l], gdn_dt_bias[l], gdn_norm_w[l], mix_w_out[l],
                   ln2_g[l], ln2_b[l])
        x = _ffn(x, ada3, ffn2_w_gate[l], ffn2_w_up[l], ffn2_w_down[l], ln3_g[l], ln3_b[l], 2)
    return x
```

```python
import functools

import jax
import jax.numpy as jnp
from jax import lax
from jax.experimental import pallas as pl
from jax.experimental.pallas import tpu as pltpu

F32 = jnp.float32
BF16 = jnp.bfloat16

N_SUB = 3
N_HEADS = 8
HEAD_DIM = 128
SGU_GROUPS = 8
SGU_BLOCK = 128
SGU_CHUNK = 64
CONV_WIDTH = 4
DEPTH = 1
MACARON_WEIGHT = 0.5
DEEPNORM_ALPHA = (2.0 * DEPTH) ** 0.25
LN_EPS = 1e-5
RMS_EPS = 1e-6
L2_EPS = 1e-6

LANES = 128
SUBLANES = 8
VMEM_LIMIT_BYTES = 56 * 1024 * 1024

MIX_TILE = 128
NEUMANN_BLOCK = 16
FFN_TILE = 1024
FFN_PARTS = 4
WEIGHT_PREP_ROWS = 128


def _dot(a, b):
    return jnp.dot(a.astype(BF16), b.astype(BF16), preferred_element_type=F32)


def _layer_norm(y, gain, bias):
    mu = jnp.mean(y, axis=-1, keepdims=True)
    yc = y - mu
    var = jnp.mean(yc * yc, axis=-1, keepdims=True)
    return yc * lax.rsqrt(var + LN_EPS) * gain + bias


def _silu(x):
    return x * jax.nn.sigmoid(x)


def _softplus(x):
    return jnp.maximum(x, 0.0) + jnp.log1p(jnp.exp(-jnp.abs(x)))


def _const_spec(shape):
    nd = len(shape)
    return pl.BlockSpec(shape, lambda *_: (0,) * nd, pipeline_mode=pl.Buffered(1))


def _ada_kernel(c_ref, w_ref, b_ref, o_ref):
    cond = _silu(c_ref[...])
    o_ref[...] = _dot(cond, w_ref[0]) + b_ref[...]


def _ada(c, w_ada, b_ada, layer):
    bsz, d = c.shape
    n = w_ada.shape[2]
    return pl.pallas_call(
        _ada_kernel,
        grid=(n // d,),
        in_specs=[
            pl.BlockSpec((bsz, d), lambda j: (0, 0)),
            pl.BlockSpec((1, d, d), lambda j: (layer, 0, j)),
            pl.BlockSpec((1, d), lambda j: (0, j)),
        ],
        out_specs=pl.BlockSpec((bsz, d), lambda j: (0, j)),
        out_shape=jax.ShapeDtypeStruct((bsz, n), F32),
        compiler_params=pltpu.CompilerParams(dimension_semantics=("arbitrary",)),
        name="ada",
    )(c, w_ada, b_ada.reshape(1, n))


def _ffn_kernel(x_ref, ada_ref, wg_ref, wu_ref, wd_ref, lng_ref, lnb_ref, o_ref, *, sub):
    shift = ada_ref[0, 3 * sub:3 * sub + 1, :]
    scale = ada_ref[0, 3 * sub + 1:3 * sub + 2, :]
    gate = ada_ref[0, 3 * sub + 2:3 * sub + 3, :]
    tm = x_ref.shape[1]
    rows = [slice(i * tm // FFN_PARTS, (i + 1) * tm // FFN_PARTS) for i in range(FFN_PARTS)]
    xs = [x_ref[0, r, :] for r in rows]
    acts = []
    for x in xs:
        h = (x * (1.0 + scale) + shift).astype(BF16)
        g = jnp.dot(h, wg_ref[...], preferred_element_type=F32)
        u = jnp.dot(h, wu_ref[...], preferred_element_type=F32)
        acts.append((_silu(g) * u).astype(BF16))
    for r, x, a in zip(rows, xs, acts):
        f = jnp.dot(a, wd_ref[...], preferred_element_type=F32)
        y = DEEPNORM_ALPHA * x + (MACARON_WEIGHT * (1.0 + gate)) * f
        o_ref[0, r, :] = _layer_norm(y, lng_ref[...], lnb_ref[...])


def _ffn(x, ada3, w_gate, w_up, w_down, ln_g, ln_b, sub):
    bsz, seq, d = x.shape
    dff = w_gate.shape[1]
    tm = FFN_TILE
    return pl.pallas_call(
        functools.partial(_ffn_kernel, sub=sub),
        grid=(bsz, seq // tm),
        in_specs=[
            pl.BlockSpec((1, tm, d), lambda b, i: (b, i, 0)),
            pl.BlockSpec((1, 3 * N_SUB, d), lambda b, i: (b, 0, 0)),
            _const_spec((d, dff)),
            _const_spec((d, dff)),
            _const_spec((dff, d)),
            _const_spec((1, d)),
            _const_spec((1, d)),
        ],
        out_specs=pl.BlockSpec((1, tm, d), lambda b, i: (b, i, 0)),
        out_shape=jax.ShapeDtypeStruct((bsz, seq, d), F32),
        compiler_params=pltpu.CompilerParams(
            dimension_semantics=("arbitrary", "arbitrary"), vmem_limit_bytes=VMEM_LIMIT_BYTES),
        name=f"ffn{sub}",
    )(x, ada3, w_gate.astype(BF16), w_up.astype(BF16), w_down.astype(BF16),
      ln_g.reshape(1, d), ln_b.reshape(1, d))


def _block_diag(pair):
    t = pair.shape[0]
    pb = pair.astype(BF16)
    zero = jnp.zeros((t, t), BF16)
    return jnp.concatenate([jnp.concatenate([pb[:, :t], zero], axis=1),
                            jnp.concatenate([zero, pb[:, t:]], axis=1)], axis=0)


def _pair_dot(x, y):
    return jnp.dot(x.astype(BF16), _block_diag(y), preferred_element_type=F32)


def _unit_lower_inverse(a_list, row, col):
    t = a_list[0].shape[0]
    eye = (row == col).astype(F32)
    same_block = (row // NEUMANN_BLOCK) == (col // NEUMANN_BLOCK)
    a_diag = [jnp.where(same_block, a, 0.0) for a in a_list]
    a_off = [a - ad for a, ad in zip(a_list, a_diag)]

    def neumann(ms, index):
        acc = [eye - m for m in ms]
        power = ms
        span = 2
        while span < index:
            power = [_pair_dot(p, p) for p in power]
            acc = [a + _pair_dot(a, p) for a, p in zip(acc, power)]
            span *= 2
        return acc

    t_diag = neumann(a_diag, NEUMANN_BLOCK)
    b = [_pair_dot(td, ao) for td, ao in zip(t_diag, a_off)]
    q = neumann(b, t // NEUMANN_BLOCK)
    return [_pair_dot(qi, td) for qi, td in zip(q, t_diag)]


def _causal_conv_silu(pre, tail, conv_w):
    rows = lax.broadcasted_iota(jnp.int32, tail.shape, 0)
    conv = pre * conv_w[CONV_WIDTH - 1:CONV_WIDTH, :]
    for s in range(1, CONV_WIDTH):
        rolled = pltpu.roll(pre, s, 0)
        head = jnp.where(rows < s, pltpu.roll(tail, s, 0), rolled[:SUBLANES, :])
        shifted = jnp.concatenate([head, rolled[SUBLANES:, :]], axis=0)
        conv = conv + shifted * conv_w[CONV_WIDTH - 1 - s:CONV_WIDTH - s, :]
    return _silu(conv)


def _mixer_kernel(x_ref, ada_ref, win_ref, wgates_ref, sgu_g_ref, sgu_b_ref, ws_ref, bs_ref, convw_ref,
                  alog_ref, dtb_ref, alogc_ref, dtbc_ref, normw_ref, wout_ref, lng_ref, lnb_ref, o_ref,
                  state_ref, tail_ref):
    nb, t, d = x_ref.shape
    width = N_HEADS * HEAD_DIM
    batches = range(nb)

    @pl.when(pl.program_id(0) == 0)
    def _():
        state_ref[...] = jnp.zeros_like(state_ref)
        tail_ref[...] = jnp.zeros_like(tail_ref)

    xs = [x_ref[b] for b in batches]
    h = jnp.concatenate(
        [(xs[b] * (1.0 + ada_ref[b, 4:5, :]) + ada_ref[b, 3:4, :]).astype(BF16) for b in batches], axis=0)
    proj_main = jnp.dot(h, win_ref[...], preferred_element_type=F32)
    proj_gates = jnp.dot(h, wgates_ref[...], preferred_element_type=F32)
    proj = [proj_main[b * t:(b + 1) * t, :] for b in batches]

    o_u, o_v, o_qkv, o_z, o_ab = 0, width, 2 * width, 5 * width, 6 * width

    row = lax.broadcasted_iota(jnp.int32, (t, t), 0)
    col = lax.broadcasted_iota(jnp.int32, (t, t), 1)
    lower = (row >= col).astype(F32)

    u_act = [jax.nn.gelu(p[:, o_u:o_u + width]) for p in proj]
    v_ln = [_layer_norm(jax.nn.gelu(p[:, o_v:o_v + width]), sgu_g_ref[...], sgu_b_ref[...]) for p in proj]
    chunk_mask = (row // SGU_CHUNK) >= (col // SGU_CHUNK)
    y_a = [[] for _ in batches]
    for g in range(SGU_GROUPS):
        w = jnp.where(chunk_mask, ws_ref[g], 0.0)
        sl = slice(g * LANES, (g + 1) * LANES)
        mixed = _dot(w, jnp.concatenate([v_ln[b][:, sl] for b in batches], axis=1))
        for b in batches:
            y_a[b].append(u_act[b][:, sl] * (mixed[:, b * LANES:(b + 1) * LANES] + bs_ref[:, g:g + 1]))

    qkv, beta_full, gc_col, gc_row = [], [], [], []
    for b in batches:
        pre = proj[b][:, o_qkv:o_qkv + 3 * width]
        tail = tail_ref[b]
        tail_ref[b] = pre[t - SUBLANES:, :]
        qkv.append(_causal_conv_silu(pre, tail, convw_ref[...]))
        ab = proj[b][:, o_ab:o_ab + LANES]
        ab_t = ab.T
        g_col = -jnp.exp(alog_ref[...]) * _softplus(ab + dtb_ref[...])
        g_row = -jnp.exp(alogc_ref[...]) * _softplus(ab_t[0:N_HEADS, :] + dtbc_ref[...])
        beta_full.append(jax.nn.sigmoid(ab))
        gc_col.append(jnp.dot(lower, g_col, preferred_element_type=F32, precision=lax.Precision.HIGHEST))
        gc_row.append(lax.dot_general(g_row, lower, (((1,), (1,)), ((), ())),
                                      preferred_element_type=F32, precision=lax.Precision.HIGHEST))

    hd2 = 2 * HEAD_DIM
    row2 = lax.broadcasted_iota(jnp.int32, (t, 2 * t), 0)
    col2 = lax.broadcasted_iota(jnp.int32, (t, 2 * t), 1) % t
    causal2 = row2 >= col2
    strict2 = row2 > col2
    pairs = [(b, j) for b in batches for j in range(N_HEADS // 2)]

    def per_half(f):
        return jnp.concatenate([f(0), f(1)], axis=1)

    def cols(x, first):
        return per_half(lambda i: jnp.broadcast_to(x[:, first + i:first + i + 1], (t, HEAD_DIM)))

    def l2_normalize(xp, scale):
        return xp * per_half(lambda i: jnp.broadcast_to(scale * lax.rsqrt(jnp.sum(
            jnp.square(xp[:, i * HEAD_DIM:(i + 1) * HEAD_DIM]), axis=-1, keepdims=True) + L2_EPS), (t, HEAD_DIM)))

    exp_gc_col = [jnp.exp(g) for g in gc_col]

    q, q_decayed, k_t, v_beta, k_beta, k_beta_g, decay, k_t_end, end_scale = [], [], [], [], [], [], [], [], []
    for b, j in pairs:
        hd = 2 * j
        lo = hd * HEAD_DIM
        qp = l2_normalize(qkv[b][:, lo:lo + hd2], HEAD_DIM ** -0.5)
        kp = l2_normalize(qkv[b][:, width + lo:width + lo + hd2], 1.0)
        vp = qkv[b][:, 2 * width + lo:2 * width + lo + hd2]
        beta = cols(beta_full[b], N_HEADS + hd)
        c = cols(gc_col[b], hd)
        r = per_half(lambda i: gc_row[b][hd + i:hd + i + 1, :])
        end = per_half(lambda i: jnp.broadcast_to(gc_row[b][hd + i:hd + i + 1, t - 1:t], (1, t)))
        ktp = per_half(lambda i: kp[:, i * HEAD_DIM:(i + 1) * HEAD_DIM].T)
        e_c = cols(exp_gc_col[b], hd)
        q.append(qp)
        q_decayed.append(qp * e_c)
        k_t.append(ktp)
        v_beta.append(vp * beta)
        k_beta.append(kp * beta)
        k_beta_g.append(kp * beta * e_c)
        decay.append(jnp.exp(jnp.where(causal2, c - r, -jnp.inf)))
        k_t_end.append(ktp * jnp.exp(end - r))
        end_scale.append(jnp.exp(end))

    n = range(len(pairs))
    a_mat = [jnp.where(strict2, _pair_dot(k_beta[p], k_t[p]) * decay[p], 0.0) for p in n]
    qk = [_pair_dot(q[p], k_t[p]) * decay[p] for p in n]
    t_inv = _unit_lower_inverse(a_mat, row2, col2)
    uw = [[_dot(t_inv[p][:, i * t:(i + 1) * t],
                jnp.concatenate([v_beta[p][:, i * HEAD_DIM:(i + 1) * HEAD_DIM],
                                 k_beta_g[p][:, i * HEAD_DIM:(i + 1) * HEAD_DIM]], axis=1))
           for i in range(2)] for p in n]
    u_val = [per_half(lambda i: uw[p][i][:, :HEAD_DIM]) for p in n]
    w_key = [per_half(lambda i: uw[p][i][:, HEAD_DIM:]) for p in n]
    state = [state_ref[p] for p in n]
    from_state = [jnp.dot(jnp.concatenate([w_key[p], q_decayed[p]], axis=0).astype(BF16),
                          _block_diag(state[p]), preferred_element_type=F32) for p in n]
    v_new = [u_val[p] - from_state[p][:t] for p in n]
    out = [from_state[p][t:] + _pair_dot(qk[p], v_new[p]) for p in n]
    for p in n:
        state_ref[p] = state[p] * end_scale[p] + _pair_dot(k_t_end[p], v_new[p])
    y_b = [[] for _ in batches]
    norm_w2 = per_half(lambda i: normw_ref[...])
    for p, (b, j) in enumerate(pairs):
        lo = o_z + 2 * j * HEAD_DIM
        o = out[p]
        o = o * per_half(lambda i: jnp.broadcast_to(lax.rsqrt(jnp.mean(
            jnp.square(o[:, i * HEAD_DIM:(i + 1) * HEAD_DIM]), axis=-1, keepdims=True) + RMS_EPS), (t, HEAD_DIM)))
        y_b[b].append(o * norm_w2 * _silu(proj[b][:, lo:lo + hd2]))

    merged = []
    for b in batches:
        gates = jax.nn.sigmoid(proj_gates[b * t:(b + 1) * t, :])
        merged.append((gates[:, :d] * jnp.concatenate(y_a[b], axis=1)
                       + gates[:, d:] * jnp.concatenate(y_b[b], axis=1)).astype(BF16))
    y = jnp.dot(jnp.concatenate(merged, axis=0), wout_ref[...], preferred_element_type=F32)
    for b in batches:
        res = DEEPNORM_ALPHA * xs[b] + (1.0 + ada_ref[b, 5:6, :]) * y[b * t:(b + 1) * t, :]
        o_ref[b] = _layer_norm(res, lng_ref[...], lnb_ref[...])


def _split_w_in_kernel(w_ref, main_ref, gates_ref, *, n_ab):
    w = w_ref[0]
    rows, n_main = main_ref.shape
    main_ref[:, :n_ab] = w[:, :n_ab].astype(BF16)
    main_ref[:, n_ab:] = jnp.zeros((rows, n_main - n_ab), BF16)
    gates_ref[...] = w[:, n_ab:].astype(BF16)


def _split_w_in(w_in, layer, n_ab, n_main):
    _, d, n = w_in.shape
    rows = WEIGHT_PREP_ROWS
    return pl.pallas_call(
        functools.partial(_split_w_in_kernel, n_ab=n_ab),
        grid=(d // rows,),
        in_specs=[pl.BlockSpec((1, rows, n), lambda i: (layer, i, 0))],
        out_specs=[pl.BlockSpec((rows, n_main), lambda i: (i, 0)),
                   pl.BlockSpec((rows, n - n_ab), lambda i: (i, 0))],
        out_shape=[jax.ShapeDtypeStruct((d, n_main), BF16), jax.ShapeDtypeStruct((d, n - n_ab), BF16)],
        compiler_params=pltpu.CompilerParams(dimension_semantics=("arbitrary",)),
        name="split_w_in",
    )(w_in)


def _mixer(x, ada3, w_in, layer, sgu_ln_g, sgu_ln_b, sgu_w_s, sgu_b_s, conv_w, a_log, dt_bias, norm_w,
           w_out, ln_g, ln_b):
    bsz, seq, d = x.shape
    width = N_HEADS * HEAD_DIM
    assert MIX_TILE == HEAD_DIM and N_HEADS % 2 == 0
    n_main = 6 * width + LANES
    w_main, w_gates = _split_w_in(w_in, layer, 6 * width + 2 * N_HEADS, n_main)

    def lane_pad(p):
        return jnp.pad(p.reshape(1, N_HEADS), ((0, 0), (0, LANES - N_HEADS)))

    t = MIX_TILE
    return pl.pallas_call(
        _mixer_kernel,
        grid=(seq // t,),
        in_specs=[
            pl.BlockSpec((bsz, t, d), lambda i: (0, i, 0)),
            _const_spec((bsz, 3 * N_SUB, d)),
            _const_spec((d, n_main)),
            _const_spec((d, 2 * d)),
            _const_spec((1, width)),
            _const_spec((1, width)),
            _const_spec((SGU_GROUPS, SGU_BLOCK, SGU_BLOCK)),
            _const_spec((SGU_BLOCK, SGU_GROUPS)),
            _const_spec((CONV_WIDTH, 3 * width)),
            _const_spec((1, LANES)),
            _const_spec((1, LANES)),
            _const_spec((N_HEADS, 1)),
            _const_spec((N_HEADS, 1)),
            _const_spec((1, HEAD_DIM)),
            _const_spec((d, d)),
            _const_spec((1, d)),
            _const_spec((1, d)),
        ],
        out_specs=pl.BlockSpec((bsz, t, d), lambda i: (0, i, 0)),
        out_shape=jax.ShapeDtypeStruct((bsz, seq, d), F32),
        scratch_shapes=[
            pltpu.VMEM((bsz * N_HEADS // 2, HEAD_DIM, 2 * HEAD_DIM), F32),
            pltpu.VMEM((bsz, SUBLANES, 3 * width), F32),
        ],
        compiler_params=pltpu.CompilerParams(
            dimension_semantics=("arbitrary",), vmem_limit_bytes=VMEM_LIMIT_BYTES),
        name="mixer",
    )(x, ada3, w_main, w_gates, sgu_ln_g.reshape(1, width), sgu_ln_b.reshape(1, width), sgu_w_s,
      sgu_b_s.T, conv_w, lane_pad(a_log), lane_pad(dt_bias), a_log.reshape(N_HEADS, 1),
      dt_bias.reshape(N_HEADS, 1), norm_w.reshape(1, HEAD_DIM), w_out.astype(BF16),
      ln_g.reshape(1, d), ln_b.reshape(1, d))


def kernel(x, c, w_ada, b_ada, ffn1_w_gate, ffn1_w_up, ffn1_w_down, ln1_g, ln1_b, mix_w_in, sgu_ln_g,
           sgu_ln_b, sgu_w_s, sgu_b_s, gdn_conv_w, gdn_a_log, gdn_dt_bias, gdn_norm_w, mix_w_out,
           ln2_g, ln2_b, ffn2_w_gate, ffn2_w_up, ffn2_w_down, ln3_g, ln3_b):
    bsz, _, d = x.shape
    for l in range(w_ada.shape[0]):
        ada3 = _ada(c, w_ada, b_ada[l], l).reshape(bsz, 3 * N_SUB, d)
        x = _ffn(x, ada3, ffn1_w_gate[l], ffn1_w_up[l], ffn1_w_down[l], ln1_g[l], ln1_b[l], 0)
        x = _mixer(x, ada3, mix_w_in, l, sgu_ln_g[l], sgu_ln_b[l], sgu_w_s[l], sgu_b_s[l],
                   gdn_conv_w[l], gdn_a_log[l], gdn_dt_bias[l], gdn_norm_w[l], mix_w_out[l],
                   ln2_g[l], ln2_b[l])
        x = _ffn(x, ada3, ffn2_w_gate[l], ffn2_w_up[l], ffn2_w_down[l], ln3_g[l], ln3_b[l], 2)
    return x
```

```python
import functools

import jax
import jax.numpy as jnp
from jax import lax
from jax.experimental import pallas as pl
from jax.experimental.pallas import tpu as pltpu

F32 = jnp.float32
BF16 = jnp.bfloat16

N_SUB = 3
N_HEADS = 8
HEAD_DIM = 128
SGU_GROUPS = 8
SGU_BLOCK = 128
SGU_CHUNK = 64
CONV_WIDTH = 4
DEPTH = 1
MACARON_WEIGHT = 0.5
DEEPNORM_ALPHA = (2.0 * DEPTH) ** 0.25
LN_EPS = 1e-5
RMS_EPS = 1e-6
L2_EPS = 1e-6

LANES = 128
SUBLANES = 8
BF16_SUBLANES = 16
VMEM_LIMIT_BYTES = 56 * 1024 * 1024

MIX_TILE = 128
NEUMANN_BLOCK = 16
FFN_TILE = 1024
FFN_PARTS = 4


def _dot(a, b):
    return jnp.dot(a.astype(BF16), b.astype(BF16), preferred_element_type=F32)


def _layer_norm(y, gain, bias):
    mu = jnp.mean(y, axis=-1, keepdims=True)
    yc = y - mu
    var = jnp.mean(yc * yc, axis=-1, keepdims=True)
    return yc * lax.rsqrt(var + LN_EPS) * gain + bias


def _silu(x):
    return x * jax.nn.sigmoid(x)


def _softplus(x):
    return jnp.maximum(x, 0.0) + jnp.log1p(jnp.exp(-jnp.abs(x)))


def _const_spec(shape):
    nd = len(shape)
    return pl.BlockSpec(shape, lambda *_: (0,) * nd, pipeline_mode=pl.Buffered(1))


def _ada_kernel(c_ref, w_ref, b_ref, o_ref):
    cond = _silu(c_ref[...])
    o_ref[...] = _dot(cond, w_ref[0]) + b_ref[...]


def _ada(c, w_ada, b_ada, layer):
    bsz, d = c.shape
    n = w_ada.shape[2]
    return pl.pallas_call(
        _ada_kernel,
        grid=(n // d,),
        in_specs=[
            pl.BlockSpec((bsz, d), lambda j: (0, 0)),
            pl.BlockSpec((1, d, d), lambda j: (layer, 0, j)),
            pl.BlockSpec((1, d), lambda j: (0, j)),
        ],
        out_specs=pl.BlockSpec((bsz, d), lambda j: (0, j)),
        out_shape=jax.ShapeDtypeStruct((bsz, n), F32),
        compiler_params=pltpu.CompilerParams(dimension_semantics=("arbitrary",)),
        name="ada",
    )(c, w_ada, b_ada.reshape(1, n))


def _ffn_kernel(x_ref, ada_ref, wg_ref, wu_ref, wd_ref, lng_ref, lnb_ref, *rest, sub, n_cast, split_at):
    n_in = n_cast + (split_at is not None)
    o_ref = rest[n_in]
    for src_ref, dst_ref in zip(rest[:n_cast], rest[n_in + 1:]):
        dst_ref[...] = src_ref[0].astype(BF16)
    if split_at is not None:
        head_ref, tail_ref = rest[n_in + 1 + n_cast:]
        w = rest[n_cast][...].astype(F32)
        rows, n_head = head_ref.shape
        head_ref[:, :split_at] = w[:, :split_at].astype(BF16)
        head_ref[:, split_at:] = jnp.zeros((rows, n_head - split_at), BF16)
        tail_ref[...] = w[:, split_at:].astype(BF16)
    shift = ada_ref[0, 3 * sub:3 * sub + 1, :]
    scale = ada_ref[0, 3 * sub + 1:3 * sub + 2, :]
    gate = ada_ref[0, 3 * sub + 2:3 * sub + 3, :]
    tm = x_ref.shape[1]
    rows = [slice(i * tm // FFN_PARTS, (i + 1) * tm // FFN_PARTS) for i in range(FFN_PARTS)]
    xs = [x_ref[0, r, :] for r in rows]
    acts = []
    for x in xs:
        h = (x * (1.0 + scale) + shift).astype(BF16)
        g = jnp.dot(h, wg_ref[...], preferred_element_type=F32)
        u = jnp.dot(h, wu_ref[...], preferred_element_type=F32)
        acts.append((_silu(g) * u).astype(BF16))
    for r, x, a in zip(rows, xs, acts):
        f = jnp.dot(a, wd_ref[...], preferred_element_type=F32)
        y = DEEPNORM_ALPHA * x + (MACARON_WEIGHT * (1.0 + gate)) * f
        o_ref[0, r, :] = _layer_norm(y, lng_ref[...], lnb_ref[...])


def _ffn(x, ada3, w_gate, w_up, w_down, ln_g, ln_b, sub, layer=0, cast_weights=(), split_weight=None):
    bsz, seq, d = x.shape
    dff = w_gate.shape[1]
    tm = FFN_TILE
    tiles = seq // tm
    steps = bsz * tiles
    in_specs = [
        pl.BlockSpec((1, tm, d), lambda b, i: (b, i, 0)),
        pl.BlockSpec((1, 3 * N_SUB, d), lambda b, i: (b, 0, 0)),
        _const_spec((d, dff)),
        _const_spec((d, dff)),
        _const_spec((dff, d)),
        _const_spec((1, d)),
        _const_spec((1, d)),
    ]
    out_specs = [pl.BlockSpec((1, tm, d), lambda b, i: (b, i, 0))]
    out_shape = [jax.ShapeDtypeStruct((bsz, seq, d), F32)]
    for w in cast_weights:
        _, rows, cols = w.shape
        chunk = rows // steps
        assert chunk * steps == rows and chunk % BF16_SUBLANES == 0
        in_specs.append(pl.BlockSpec((1, chunk, cols), lambda b, i: (layer, b * tiles + i, 0)))
        out_specs.append(pl.BlockSpec((chunk, cols), lambda b, i: (b * tiles + i, 0)))
        out_shape.append(jax.ShapeDtypeStruct((rows, cols), BF16))
    extra = tuple(cast_weights)
    split_at = None
    if split_weight is not None:
        w, split_at, n_head = split_weight
        rows, cols = w.shape
        chunk = rows // steps
        assert chunk * steps == rows and chunk % BF16_SUBLANES == 0
        in_specs.append(pl.BlockSpec((chunk, cols), lambda b, i: (b * tiles + i, 0)))
        for n in (n_head, cols - split_at):
            out_specs.append(pl.BlockSpec((chunk, n), lambda b, i: (b * tiles + i, 0)))
            out_shape.append(jax.ShapeDtypeStruct((rows, n), BF16))
        extra += (w,)
    return pl.pallas_call(
        functools.partial(_ffn_kernel, sub=sub, n_cast=len(cast_weights), split_at=split_at),
        grid=(bsz, tiles),
        in_specs=in_specs,
        out_specs=out_specs,
        out_shape=out_shape,
        compiler_params=pltpu.CompilerParams(
            dimension_semantics=("arbitrary", "arbitrary"), vmem_limit_bytes=VMEM_LIMIT_BYTES),
        name=f"ffn{sub}",
    )(x, ada3, w_gate.astype(BF16), w_up.astype(BF16), w_down.astype(BF16),
      ln_g.reshape(1, d), ln_b.reshape(1, d), *extra)


def _block_diag(pair):
    t = pair.shape[0]
    pb = pair.astype(BF16)
    zero = jnp.zeros((t, t), BF16)
    return jnp.concatenate([jnp.concatenate([pb[:, :t], zero], axis=1),
                            jnp.concatenate([zero, pb[:, t:]], axis=1)], axis=0)


def _pair_dot(x, y):
    return jnp.dot(x.astype(BF16), _block_diag(y), preferred_element_type=F32)


def _unit_lower_inverse(a_list, row, col):
    t = a_list[0].shape[0]
    eye = (row == col).astype(F32)
    same_block = (row // NEUMANN_BLOCK) == (col // NEUMANN_BLOCK)
    a_diag = [jnp.where(same_block, a, 0.0) for a in a_list]
    a_off = [a - ad for a, ad in zip(a_list, a_diag)]

    def neumann(ms, index):
        acc = [eye - m for m in ms]
        power = ms
        span = 2
        while span < index:
            power = [_pair_dot(p, p) for p in power]
            acc = [a + _pair_dot(a, p) for a, p in zip(acc, power)]
            span *= 2
        return acc

    t_diag = neumann(a_diag, NEUMANN_BLOCK)
    b = [_pair_dot(td, ao) for td, ao in zip(t_diag, a_off)]
    q = neumann(b, t // NEUMANN_BLOCK)
    return [_pair_dot(qi, td) for qi, td in zip(q, t_diag)]


def _causal_conv_silu(pre, tail, conv_w):
    rows = lax.broadcasted_iota(jnp.int32, tail.shape, 0)
    conv = pre * conv_w[CONV_WIDTH - 1:CONV_WIDTH, :]
    for s in range(1, CONV_WIDTH):
        rolled = pltpu.roll(pre, s, 0)
        head = jnp.where(rows < s, pltpu.roll(tail, s, 0), rolled[:SUBLANES, :])
        shifted = jnp.concatenate([head, rolled[SUBLANES:, :]], axis=0)
        conv = conv + shifted * conv_w[CONV_WIDTH - 1 - s:CONV_WIDTH - s, :]
    return _silu(conv)


def _mixer_kernel(x_ref, ada_ref, win_ref, wgates_ref, sgu_g_ref, sgu_b_ref, ws_ref, bs_ref, convw_ref,
                  alog_ref, dtb_ref, alogc_ref, dtbc_ref, normw_ref, wout_ref, lng_ref, lnb_ref, o_ref,
                  state_ref, tail_ref):
    nb, t, d = x_ref.shape
    width = N_HEADS * HEAD_DIM
    batches = range(nb)

    @pl.when(pl.program_id(0) == 0)
    def _():
        state_ref[...] = jnp.zeros_like(state_ref)
        tail_ref[...] = jnp.zeros_like(tail_ref)

    xs = [x_ref[b] for b in batches]
    h = jnp.concatenate(
        [(xs[b] * (1.0 + ada_ref[b, 4:5, :]) + ada_ref[b, 3:4, :]).astype(BF16) for b in batches], axis=0)
    proj_main = jnp.dot(h, win_ref[...], preferred_element_type=F32)
    proj_gates = jnp.dot(h, wgates_ref[...], preferred_element_type=F32)
    proj = [proj_main[b * t:(b + 1) * t, :] for b in batches]

    o_u, o_v, o_qkv, o_z, o_ab = 0, width, 2 * width, 5 * width, 6 * width

    row = lax.broadcasted_iota(jnp.int32, (t, t), 0)
    col = lax.broadcasted_iota(jnp.int32, (t, t), 1)
    lower = (row >= col).astype(F32)

    u_act = [jax.nn.gelu(p[:, o_u:o_u + width]) for p in proj]
    v_ln = [_layer_norm(jax.nn.gelu(p[:, o_v:o_v + width]), sgu_g_ref[...], sgu_b_ref[...]) for p in proj]
    chunk_mask = (row // SGU_CHUNK) >= (col // SGU_CHUNK)
    y_a = [[] for _ in batches]
    for g in range(SGU_GROUPS):
        w = jnp.where(chunk_mask, ws_ref[g], 0.0)
        sl = slice(g * LANES, (g + 1) * LANES)
        mixed = _dot(w, jnp.concatenate([v_ln[b][:, sl] for b in batches], axis=1))
        for b in batches:
            y_a[b].append(u_act[b][:, sl] * (mixed[:, b * LANES:(b + 1) * LANES] + bs_ref[:, g:g + 1]))

    qkv, beta_full, gc_col, gc_row = [], [], [], []
    for b in batches:
        pre = proj[b][:, o_qkv:o_qkv + 3 * width]
        tail = tail_ref[b]
        tail_ref[b] = pre[t - SUBLANES:, :]
        qkv.append(_causal_conv_silu(pre, tail, convw_ref[...]))
        ab = proj[b][:, o_ab:o_ab + LANES]
        ab_t = ab.T
        g_col = -jnp.exp(alog_ref[...]) * _softplus(ab + dtb_ref[...])
        g_row = -jnp.exp(alogc_ref[...]) * _softplus(ab_t[0:N_HEADS, :] + dtbc_ref[...])
        beta_full.append(jax.nn.sigmoid(ab))
        gc_col.append(jnp.dot(lower, g_col, preferred_element_type=F32, precision=lax.Precision.HIGHEST))
        gc_row.append(lax.dot_general(g_row, lower, (((1,), (1,)), ((), ())),
                                      preferred_element_type=F32, precision=lax.Precision.HIGHEST))

    hd2 = 2 * HEAD_DIM
    row2 = lax.broadcasted_iota(jnp.int32, (t, 2 * t), 0)
    col2 = lax.broadcasted_iota(jnp.int32, (t, 2 * t), 1) % t
    causal2 = row2 >= col2
    strict2 = row2 > col2
    pairs = [(b, j) for b in batches for j in range(N_HEADS // 2)]

    def per_half(f):
        return jnp.concatenate([f(0), f(1)], axis=1)

    def cols(x, first):
        return per_half(lambda i: jnp.broadcast_to(x[:, first + i:first + i + 1], (t, HEAD_DIM)))

    def l2_normalize(xp, scale):
        return xp * per_half(lambda i: jnp.broadcast_to(scale * lax.rsqrt(jnp.sum(
            jnp.square(xp[:, i * HEAD_DIM:(i + 1) * HEAD_DIM]), axis=-1, keepdims=True) + L2_EPS), (t, HEAD_DIM)))

    exp_gc_col = [jnp.exp(g) for g in gc_col]

    q, q_decayed, k_t, v_beta, k_beta, k_beta_g, decay, k_t_end, end_scale = [], [], [], [], [], [], [], [], []
    for b, j in pairs:
        hd = 2 * j
        lo = hd * HEAD_DIM
        qp = l2_normalize(qkv[b][:, lo:lo + hd2], HEAD_DIM ** -0.5)
        kp = l2_normalize(qkv[b][:, width + lo:width + lo + hd2], 1.0)
        vp = qkv[b][:, 2 * width + lo:2 * width + lo + hd2]
        beta = cols(beta_full[b], N_HEADS + hd)
        c = cols(gc_col[b], hd)
        r = per_half(lambda i: gc_row[b][hd + i:hd + i + 1, :])
        end = per_half(lambda i: jnp.broadcast_to(gc_row[b][hd + i:hd + i + 1, t - 1:t], (1, t)))
        ktp = per_half(lambda i: kp[:, i * HEAD_DIM:(i + 1) * HEAD_DIM].T)
        e_c = cols(exp_gc_col[b], hd)
        q.append(qp)
        q_decayed.append(qp * e_c)
        k_t.append(ktp)
        v_beta.append(vp * beta)
        k_beta.append(kp * beta)
        k_beta_g.append(kp * beta * e_c)
        decay.append(jnp.exp(jnp.where(causal2, c - r, -jnp.inf)))
        k_t_end.append(ktp * jnp.exp(end - r))
        end_scale.append(jnp.exp(end))

    n = range(len(pairs))
    a_mat = [jnp.where(strict2, _pair_dot(k_beta[p], k_t[p]) * decay[p], 0.0) for p in n]
    qk = [_pair_dot(q[p], k_t[p]) * decay[p] for p in n]
    t_inv = _unit_lower_inverse(a_mat, row2, col2)
    uw = [[_dot(t_inv[p][:, i * t:(i + 1) * t],
                jnp.concatenate([v_beta[p][:, i * HEAD_DIM:(i + 1) * HEAD_DIM],
                                 k_beta_g[p][:, i * HEAD_DIM:(i + 1) * HEAD_DIM]], axis=1))
           for i in range(2)] for p in n]
    u_val = [per_half(lambda i: uw[p][i][:, :HEAD_DIM]) for p in n]
    w_key = [per_half(lambda i: uw[p][i][:, HEAD_DIM:]) for p in n]
    state = [state_ref[p] for p in n]
    from_state = [jnp.dot(jnp.concatenate([w_key[p], q_decayed[p]], axis=0).astype(BF16),
                          _block_diag(state[p]), preferred_element_type=F32) for p in n]
    v_new = [u_val[p] - from_state[p][:t] for p in n]
    out = [from_state[p][t:] + _pair_dot(qk[p], v_new[p]) for p in n]
    for p in n:
        state_ref[p] = state[p] * end_scale[p] + _pair_dot(k_t_end[p], v_new[p])
    y_b = [[] for _ in batches]
    norm_w2 = per_half(lambda i: normw_ref[...])
    for p, (b, j) in enumerate(pairs):
        lo = o_z + 2 * j * HEAD_DIM
        o = out[p]
        o = o * per_half(lambda i: jnp.broadcast_to(lax.rsqrt(jnp.mean(
            jnp.square(o[:, i * HEAD_DIM:(i + 1) * HEAD_DIM]), axis=-1, keepdims=True) + RMS_EPS), (t, HEAD_DIM)))
        y_b[b].append(o * norm_w2 * _silu(proj[b][:, lo:lo + hd2]))

    merged = []
    for b in batches:
        gates = jax.nn.sigmoid(proj_gates[b * t:(b + 1) * t, :])
        merged.append((gates[:, :d] * jnp.concatenate(y_a[b], axis=1)
                       + gates[:, d:] * jnp.concatenate(y_b[b], axis=1)).astype(BF16))
    y = jnp.dot(jnp.concatenate(merged, axis=0), wout_ref[...], preferred_element_type=F32)
    for b in batches:
        res = DEEPNORM_ALPHA * xs[b] + (1.0 + ada_ref[b, 5:6, :]) * y[b * t:(b + 1) * t, :]
        o_ref[b] = _layer_norm(res, lng_ref[...], lnb_ref[...])


def _mixer(x, ada3, w_main, w_gates, sgu_ln_g, sgu_ln_b, sgu_w_s, sgu_b_s, conv_w, a_log, dt_bias, norm_w,
           w_out, ln_g, ln_b):
    bsz, seq, d = x.shape
    width = N_HEADS * HEAD_DIM
    assert MIX_TILE == HEAD_DIM and N_HEADS % 2 == 0
    n_main = 6 * width + LANES
    assert w_main.shape == (d, n_main) and w_gates.shape == (d, 2 * d)

    def lane_pad(p):
        return jnp.pad(p.reshape(1, N_HEADS), ((0, 0), (0, LANES - N_HEADS)))

    t = MIX_TILE
    return pl.pallas_call(
        _mixer_kernel,
        grid=(seq // t,),
        in_specs=[
            pl.BlockSpec((bsz, t, d), lambda i: (0, i, 0)),
            _const_spec((bsz, 3 * N_SUB, d)),
            _const_spec((d, n_main)),
            _const_spec((d, 2 * d)),
            _const_spec((1, width)),
            _const_spec((1, width)),
            _const_spec((SGU_GROUPS, SGU_BLOCK, SGU_BLOCK)),
            _const_spec((SGU_BLOCK, SGU_GROUPS)),
            _const_spec((CONV_WIDTH, 3 * width)),
            _const_spec((1, LANES)),
            _const_spec((1, LANES)),
            _const_spec((N_HEADS, 1)),
            _const_spec((N_HEADS, 1)),
            _const_spec((1, HEAD_DIM)),
            _const_spec((d, d)),
            _const_spec((1, d)),
            _const_spec((1, d)),
        ],
        out_specs=pl.BlockSpec((bsz, t, d), lambda i: (0, i, 0)),
        out_shape=jax.ShapeDtypeStruct((bsz, seq, d), F32),
        scratch_shapes=[
            pltpu.VMEM((bsz * N_HEADS // 2, HEAD_DIM, 2 * HEAD_DIM), F32),
            pltpu.VMEM((bsz, SUBLANES, 3 * width), F32),
        ],
        compiler_params=pltpu.CompilerParams(
            dimension_semantics=("arbitrary",), vmem_limit_bytes=VMEM_LIMIT_BYTES),
        name="mixer",
    )(x, ada3, w_main, w_gates, sgu_ln_g.reshape(1, width), sgu_ln_b.reshape(1, width), sgu_w_s,
      sgu_b_s.T, conv_w, lane_pad(a_log), lane_pad(dt_bias), a_log.reshape(N_HEADS, 1),
      dt_bias.reshape(N_HEADS, 1), norm_w.reshape(1, HEAD_DIM), w_out.astype(BF16),
      ln_g.reshape(1, d), ln_b.reshape(1, d))


def kernel(x, c, w_ada, b_ada, ffn1_w_gate, ffn1_w_up, ffn1_w_down, ln1_g, ln1_b, mix_w_in, sgu_ln_g,
           sgu_ln_b, sgu_w_s, sgu_b_s, gdn_conv_w, gdn_a_log, gdn_dt_bias, gdn_norm_w, mix_w_out,
           ln2_g, ln2_b, ffn2_w_gate, ffn2_w_up, ffn2_w_down, ln3_g, ln3_b):
    bsz, _, d = x.shape
    for l in range(w_ada.shape[0]):
        ada3 = _ada(c, w_ada, b_ada[l], l).reshape(bsz, 3 * N_SUB, d)
        width = N_HEADS * HEAD_DIM
        x, w2_gate, w2_up, w2_down, w_out, w_main, w_gates = _ffn(
            x, ada3, ffn1_w_gate[l], ffn1_w_up[l], ffn1_w_down[l], ln1_g[l], ln1_b[l], 0, layer=l,
            cast_weights=(ffn2_w_gate, ffn2_w_up, ffn2_w_down, mix_w_out),
            split_weight=(mix_w_in[l].astype(BF16), 6 * width + 2 * N_HEADS, 6 * width + LANES))
        x = _mixer(x, ada3, w_main, w_gates, sgu_ln_g[l], sgu_ln_b[l], sgu_w_s[l], sgu_b_s[l],
                   gdn_conv_w[l], gdn_a_log[l], gdn_dt_bias[l], gdn_norm_w[l], w_out,
                   ln2_g[l], ln2_b[l])
        x, = _ffn(x, ada3, w2_gate, w2_up, w2_down, ln3_g[l], ln3_b[l], 2)
    return x
```

```python
import functools

import jax
import jax.numpy as jnp
from jax import lax
from jax.experimental import pallas as pl
from jax.experimental.pallas import tpu as pltpu

F32 = jnp.float32
BF16 = jnp.bfloat16

N_SUB = 3
N_HEADS = 8
HEAD_DIM = 128
SGU_GROUPS = 8
SGU_BLOCK = 128
SGU_CHUNK = 64
CONV_WIDTH = 4
DEPTH = 1
MACARON_WEIGHT = 0.5
DEEPNORM_ALPHA = (2.0 * DEPTH) ** 0.25
LN_EPS = 1e-5
RMS_EPS = 1e-6
L2_EPS = 1e-6

LANES = 128
SUBLANES = 8
BF16_SUBLANES = 16
VMEM_LIMIT_BYTES = 56 * 1024 * 1024

MIX_TILE = 128
NEUMANN_BLOCK = 4
FFN_TILE = 1024
FFN_PARTS = 4


def _dot(a, b):
    return jnp.dot(a.astype(BF16), b.astype(BF16), preferred_element_type=F32)


def _layer_norm(y, gain, bias):
    mu = jnp.mean(y, axis=-1, keepdims=True)
    yc = y - mu
    var = jnp.mean(yc * yc, axis=-1, keepdims=True)
    return yc * lax.rsqrt(var + LN_EPS) * gain + bias


def _silu(x):
    return x * jax.nn.sigmoid(x)


def _softplus(x):
    return jnp.maximum(x, 0.0) + jnp.log1p(jnp.exp(-jnp.abs(x)))


def _const_spec(shape):
    nd = len(shape)
    return pl.BlockSpec(shape, lambda *_: (0,) * nd, pipeline_mode=pl.Buffered(1))


def _ada_kernel(c_ref, w_ref, b_ref, o_ref):
    cond = _silu(c_ref[...])
    o_ref[...] = _dot(cond, w_ref[0]) + b_ref[...]


def _ada(c, w_ada, b_ada, layer):
    bsz, d = c.shape
    n = w_ada.shape[2]
    return pl.pallas_call(
        _ada_kernel,
        grid=(n // d,),
        in_specs=[
            pl.BlockSpec((bsz, d), lambda j: (0, 0)),
            pl.BlockSpec((1, d, d), lambda j: (layer, 0, j)),
            pl.BlockSpec((1, d), lambda j: (0, j)),
        ],
        out_specs=pl.BlockSpec((bsz, d), lambda j: (0, j)),
        out_shape=jax.ShapeDtypeStruct((bsz, n), F32),
        compiler_params=pltpu.CompilerParams(dimension_semantics=("arbitrary",)),
        name="ada",
    )(c, w_ada, b_ada.reshape(1, n))


def _ffn_kernel(x_ref, ada_ref, wg_ref, wu_ref, wd_ref, lng_ref, lnb_ref, *rest, sub, n_cast, split_at):
    n_in = n_cast + (split_at is not None)
    o_ref = rest[n_in]
    for src_ref, dst_ref in zip(rest[:n_cast], rest[n_in + 1:]):
        dst_ref[...] = src_ref[0].astype(BF16)
    if split_at is not None:
        head_ref, tail_ref = rest[n_in + 1 + n_cast:]
        w = rest[n_cast][...].astype(F32)
        rows, n_head = head_ref.shape
        head_ref[:, :split_at] = w[:, :split_at].astype(BF16)
        head_ref[:, split_at:] = jnp.zeros((rows, n_head - split_at), BF16)
        tail_ref[...] = w[:, split_at:].astype(BF16)
    shift = ada_ref[0, 3 * sub:3 * sub + 1, :]
    scale = ada_ref[0, 3 * sub + 1:3 * sub + 2, :]
    gate = ada_ref[0, 3 * sub + 2:3 * sub + 3, :]
    tm = x_ref.shape[1]
    rows = [slice(i * tm // FFN_PARTS, (i + 1) * tm // FFN_PARTS) for i in range(FFN_PARTS)]
    xs = [x_ref[0, r, :] for r in rows]
    acts = []
    for x in xs:
        h = (x * (1.0 + scale) + shift).astype(BF16)
        g = jnp.dot(h, wg_ref[...], preferred_element_type=F32)
        u = jnp.dot(h, wu_ref[...], preferred_element_type=F32)
        acts.append((_silu(g) * u).astype(BF16))
    for r, x, a in zip(rows, xs, acts):
        f = jnp.dot(a, wd_ref[...], preferred_element_type=F32)
        y = DEEPNORM_ALPHA * x + (MACARON_WEIGHT * (1.0 + gate)) * f
        o_ref[0, r, :] = _layer_norm(y, lng_ref[...], lnb_ref[...])


def _ffn(x, ada3, w_gate, w_up, w_down, ln_g, ln_b, sub, layer=0, cast_weights=(), split_weight=None):
    bsz, seq, d = x.shape
    dff = w_gate.shape[1]
    tm = FFN_TILE
    tiles = seq // tm
    steps = bsz * tiles
    in_specs = [
        pl.BlockSpec((1, tm, d), lambda b, i: (b, i, 0)),
        pl.BlockSpec((1, 3 * N_SUB, d), lambda b, i: (b, 0, 0)),
        _const_spec((d, dff)),
        _const_spec((d, dff)),
        _const_spec((dff, d)),
        _const_spec((1, d)),
        _const_spec((1, d)),
    ]
    out_specs = [pl.BlockSpec((1, tm, d), lambda b, i: (b, i, 0))]
    out_shape = [jax.ShapeDtypeStruct((bsz, seq, d), F32)]
    for w in cast_weights:
        _, rows, cols = w.shape
        chunk = rows // steps
        assert chunk * steps == rows and chunk % BF16_SUBLANES == 0
        in_specs.append(pl.BlockSpec((1, chunk, cols), lambda b, i: (layer, b * tiles + i, 0)))
        out_specs.append(pl.BlockSpec((chunk, cols), lambda b, i: (b * tiles + i, 0)))
        out_shape.append(jax.ShapeDtypeStruct((rows, cols), BF16))
    extra = tuple(cast_weights)
    split_at = None
    if split_weight is not None:
        w, split_at, n_head = split_weight
        rows, cols = w.shape
        chunk = rows // steps
        assert chunk * steps == rows and chunk % BF16_SUBLANES == 0
        in_specs.append(pl.BlockSpec((chunk, cols), lambda b, i: (b * tiles + i, 0)))
        for n in (n_head, cols - split_at):
            out_specs.append(pl.BlockSpec((chunk, n), lambda b, i: (b * tiles + i, 0)))
            out_shape.append(jax.ShapeDtypeStruct((rows, n), BF16))
        extra += (w,)
    return pl.pallas_call(
        functools.partial(_ffn_kernel, sub=sub, n_cast=len(cast_weights), split_at=split_at),
        grid=(bsz, tiles),
        in_specs=in_specs,
        out_specs=out_specs,
        out_shape=out_shape,
        compiler_params=pltpu.CompilerParams(
            dimension_semantics=("arbitrary", "arbitrary"), vmem_limit_bytes=VMEM_LIMIT_BYTES),
        name=f"ffn{sub}",
    )(x, ada3, w_gate.astype(BF16), w_up.astype(BF16), w_down.astype(BF16),
      ln_g.reshape(1, d), ln_b.reshape(1, d), *extra)


def _block_diag(pair):
    t = pair.shape[0]
    pb = pair.astype(BF16)
    zero = jnp.zeros((t, t), BF16)
    return jnp.concatenate([jnp.concatenate([pb[:, :t], zero], axis=1),
                            jnp.concatenate([zero, pb[:, t:]], axis=1)], axis=0)


def _pair_dot(x, y):
    return jnp.dot(x.astype(BF16), _block_diag(y), preferred_element_type=F32)


def _unit_lower_inverse(a_list, row, col):
    t = a_list[0].shape[0]
    eye = (row == col).astype(F32)

    def same_block(n):
        return (row // n) == (col // n)

    a_diag = [jnp.where(same_block(NEUMANN_BLOCK), a, 0.0) for a in a_list]
    inv = [eye - m for m in a_diag]
    power = a_diag
    span = 2
    while span < NEUMANN_BLOCK:
        power = [_pair_dot(p, p) for p in power]
        inv = [i + _pair_dot(i, p) for i, p in zip(inv, power)]
        span *= 2
    n = NEUMANN_BLOCK
    while n < t:
        joins = same_block(2 * n) & jnp.logical_not(same_block(n))
        tx = [_pair_dot(i, jnp.where(joins, a, 0.0)) for i, a in zip(inv, a_list)]
        inv = [i - _pair_dot(txi, i) for i, txi in zip(inv, tx)]
        n *= 2
    return inv


def _causal_conv_silu(pre, tail, conv_w):
    rows = lax.broadcasted_iota(jnp.int32, tail.shape, 0)
    conv = pre * conv_w[CONV_WIDTH - 1:CONV_WIDTH, :]
    for s in range(1, CONV_WIDTH):
        rolled = pltpu.roll(pre, s, 0)
        head = jnp.where(rows < s, pltpu.roll(tail, s, 0), rolled[:SUBLANES, :])
        shifted = jnp.concatenate([head, rolled[SUBLANES:, :]], axis=0)
        conv = conv + shifted * conv_w[CONV_WIDTH - 1 - s:CONV_WIDTH - s, :]
    return _silu(conv)


def _mixer_kernel(x_ref, ada_ref, win_ref, wgates_ref, sgu_g_ref, sgu_b_ref, ws_ref, bs_ref, convw_ref,
                  alog_ref, dtb_ref, alogc_ref, dtbc_ref, normw_ref, wout_ref, lng_ref, lnb_ref, o_ref,
                  state_ref, tail_ref):
    nb, t, d = x_ref.shape
    width = N_HEADS * HEAD_DIM
    batches = range(nb)

    @pl.when(pl.program_id(0) == 0)
    def _():
        state_ref[...] = jnp.zeros_like(state_ref)
        tail_ref[...] = jnp.zeros_like(tail_ref)

    xs = [x_ref[b] for b in batches]
    h = jnp.concatenate(
        [(xs[b] * (1.0 + ada_ref[b, 4:5, :]) + ada_ref[b, 3:4, :]).astype(BF16) for b in batches], axis=0)
    proj_main = jnp.dot(h, win_ref[...], preferred_element_type=F32)
    proj_gates = jnp.dot(h, wgates_ref[...], preferred_element_type=F32)
    proj = [proj_main[b * t:(b + 1) * t, :] for b in batches]

    o_u, o_v, o_qkv, o_z, o_ab = 0, width, 2 * width, 5 * width, 6 * width

    row = lax.broadcasted_iota(jnp.int32, (t, t), 0)
    col = lax.broadcasted_iota(jnp.int32, (t, t), 1)
    lower = (row >= col).astype(F32)

    u_act = [jax.nn.gelu(p[:, o_u:o_u + width]) for p in proj]
    v_ln = [_layer_norm(jax.nn.gelu(p[:, o_v:o_v + width]), sgu_g_ref[...], sgu_b_ref[...]) for p in proj]
    chunk_mask = (row // SGU_CHUNK) >= (col // SGU_CHUNK)
    y_a = [[] for _ in batches]
    for g in range(SGU_GROUPS):
        w = jnp.where(chunk_mask, ws_ref[g], 0.0)
        sl = slice(g * LANES, (g + 1) * LANES)
        mixed = _dot(w, jnp.concatenate([v_ln[b][:, sl] for b in batches], axis=1))
        for b in batches:
            y_a[b].append(u_act[b][:, sl] * (mixed[:, b * LANES:(b + 1) * LANES] + bs_ref[:, g:g + 1]))

    qkv, beta_full, gc_col, gc_row = [], [], [], []
    for b in batches:
        pre = proj[b][:, o_qkv:o_qkv + 3 * width]
        tail = tail_ref[b]
        tail_ref[b] = pre[t - SUBLANES:, :]
        qkv.append(_causal_conv_silu(pre, tail, convw_ref[...]))
        ab = proj[b][:, o_ab:o_ab + LANES]
        ab_t = ab.T
        g_col = -jnp.exp(alog_ref[...]) * _softplus(ab + dtb_ref[...])
        g_row = -jnp.exp(alogc_ref[...]) * _softplus(ab_t[0:N_HEADS, :] + dtbc_ref[...])
        beta_full.append(jax.nn.sigmoid(ab))
        gc_col.append(jnp.dot(lower, g_col, preferred_element_type=F32, precision=lax.Precision.HIGHEST))
        gc_row.append(lax.dot_general(g_row, lower, (((1,), (1,)), ((), ())),
                                      preferred_element_type=F32, precision=lax.Precision.HIGHEST))

    hd2 = 2 * HEAD_DIM
    row2 = lax.broadcasted_iota(jnp.int32, (t, 2 * t), 0)
    col2 = lax.broadcasted_iota(jnp.int32, (t, 2 * t), 1) % t
    causal2 = row2 >= col2
    strict2 = row2 > col2
    pairs = [(b, j) for b in batches for j in range(N_HEADS // 2)]

    def per_half(f):
        return jnp.concatenate([f(0), f(1)], axis=1)

    def cols(x, first):
        return per_half(lambda i: jnp.broadcast_to(x[:, first + i:first + i + 1], (t, HEAD_DIM)))

    def l2_normalize(xp, scale):
        return xp * per_half(lambda i: jnp.broadcast_to(scale * lax.rsqrt(jnp.sum(
            jnp.square(xp[:, i * HEAD_DIM:(i + 1) * HEAD_DIM]), axis=-1, keepdims=True) + L2_EPS), (t, HEAD_DIM)))

    exp_gc_col = [jnp.exp(g) for g in gc_col]

    q, q_decayed, k_t, v_beta, k_beta, k_beta_g, decay, k_t_end, end_scale = [], [], [], [], [], [], [], [], []
    for b, j in pairs:
        hd = 2 * j
        lo = hd * HEAD_DIM
        qp = l2_normalize(qkv[b][:, lo:lo + hd2], HEAD_DIM ** -0.5)
        kp = l2_normalize(qkv[b][:, width + lo:width + lo + hd2], 1.0)
        vp = qkv[b][:, 2 * width + lo:2 * width + lo + hd2]
        beta = cols(beta_full[b], N_HEADS + hd)
        c = cols(gc_col[b], hd)
        r = per_half(lambda i: gc_row[b][hd + i:hd + i + 1, :])
        end = per_half(lambda i: jnp.broadcast_to(gc_row[b][hd + i:hd + i + 1, t - 1:t], (1, t)))
        ktp = per_half(lambda i: kp[:, i * HEAD_DIM:(i + 1) * HEAD_DIM].T)
        e_c = cols(exp_gc_col[b], hd)
        q.append(qp)
        q_decayed.append(qp * e_c)
        k_t.append(ktp)
        v_beta.append(vp * beta)
        k_beta.append(kp * beta)
        k_beta_g.append(kp * beta * e_c)
        decay.append(jnp.exp(jnp.where(causal2, c - r, -jnp.inf)))
        k_t_end.append(ktp * jnp.exp(end - r))
        end_scale.append(jnp.exp(end))

    n = range(len(pairs))
    a_mat = [jnp.where(strict2, _pair_dot(k_beta[p], k_t[p]) * decay[p], 0.0) for p in n]
    qk = [_pair_dot(q[p], k_t[p]) * decay[p] for p in n]
    t_inv = _unit_lower_inverse(a_mat, row2, col2)
    uw = [[_dot(t_inv[p][:, i * t:(i + 1) * t],
                jnp.concatenate([v_beta[p][:, i * HEAD_DIM:(i + 1) * HEAD_DIM],
                                 k_beta_g[p][:, i * HEAD_DIM:(i + 1) * HEAD_DIM]], axis=1))
           for i in range(2)] for p in n]
    u_val = [per_half(lambda i: uw[p][i][:, :HEAD_DIM]) for p in n]
    w_key = [per_half(lambda i: uw[p][i][:, HEAD_DIM:]) for p in n]
    state = [state_ref[p] for p in n]
    from_state = [jnp.dot(jnp.concatenate([w_key[p], q_decayed[p]], axis=0).astype(BF16),
                          _block_diag(state[p]), preferred_element_type=F32) for p in n]
    v_new = [u_val[p] - from_state[p][:t] for p in n]
    out = [from_state[p][t:] + _pair_dot(qk[p], v_new[p]) for p in n]
    for p in n:
        state_ref[p] = state[p] * end_scale[p] + _pair_dot(k_t_end[p], v_new[p])
    y_b = [[] for _ in batches]
    norm_w2 = per_half(lambda i: normw_ref[...])
    for p, (b, j) in enumerate(pairs):
        lo = o_z + 2 * j * HEAD_DIM
        o = out[p]
        o = o * per_half(lambda i: jnp.broadcast_to(lax.rsqrt(jnp.mean(
            jnp.square(o[:, i * HEAD_DIM:(i + 1) * HEAD_DIM]), axis=-1, keepdims=True) + RMS_EPS), (t, HEAD_DIM)))
        y_b[b].append(o * norm_w2 * _silu(proj[b][:, lo:lo + hd2]))

    merged = []
    for b in batches:
        gates = jax.nn.sigmoid(proj_gates[b * t:(b + 1) * t, :])
        merged.append((gates[:, :d] * jnp.concatenate(y_a[b], axis=1)
                       + gates[:, d:] * jnp.concatenate(y_b[b], axis=1)).astype(BF16))
    y = jnp.dot(jnp.concatenate(merged, axis=0), wout_ref[...], preferred_element_type=F32)
    for b in batches:
        res = DEEPNORM_ALPHA * xs[b] + (1.0 + ada_ref[b, 5:6, :]) * y[b * t:(b + 1) * t, :]
        o_ref[b] = _layer_norm(res, lng_ref[...], lnb_ref[...])


def _mixer(x, ada3, w_main, w_gates, sgu_ln_g, sgu_ln_b, sgu_w_s, sgu_b_s, conv_w, a_log, dt_bias, norm_w,
           w_out, ln_g, ln_b):
    bsz, seq, d = x.shape
    width = N_HEADS * HEAD_DIM
    assert MIX_TILE == HEAD_DIM and N_HEADS % 2 == 0
    n_main = 6 * width + LANES
    assert w_main.shape == (d, n_main) and w_gates.shape == (d, 2 * d)

    def lane_pad(p):
        return jnp.pad(p.reshape(1, N_HEADS), ((0, 0), (0, LANES - N_HEADS)))

    t = MIX_TILE
    return pl.pallas_call(
        _mixer_kernel,
        grid=(seq // t,),
        in_specs=[
            pl.BlockSpec((bsz, t, d), lambda i: (0, i, 0)),
            _const_spec((bsz, 3 * N_SUB, d)),
            _const_spec((d, n_main)),
            _const_spec((d, 2 * d)),
            _const_spec((1, width)),
            _const_spec((1, width)),
            _const_spec((SGU_GROUPS, SGU_BLOCK, SGU_BLOCK)),
            _const_spec((SGU_BLOCK, SGU_GROUPS)),
            _const_spec((CONV_WIDTH, 3 * width)),
            _const_spec((1, LANES)),
            _const_spec((1, LANES)),
            _const_spec((N_HEADS, 1)),
            _const_spec((N_HEADS, 1)),
            _const_spec((1, HEAD_DIM)),
            _const_spec((d, d)),
            _const_spec((1, d)),
            _const_spec((1, d)),
        ],
        out_specs=pl.BlockSpec((bsz, t, d), lambda i: (0, i, 0)),
        out_shape=jax.ShapeDtypeStruct((bsz, seq, d), F32),
        scratch_shapes=[
            pltpu.VMEM((bsz * N_HEADS // 2, HEAD_DIM, 2 * HEAD_DIM), F32),
            pltpu.VMEM((bsz, SUBLANES, 3 * width), F32),
        ],
        compiler_params=pltpu.CompilerParams(
            dimension_semantics=("arbitrary",), vmem_limit_bytes=VMEM_LIMIT_BYTES),
        name="mixer",
    )(x, ada3, w_main, w_gates, sgu_ln_g.reshape(1, width), sgu_ln_b.reshape(1, width), sgu_w_s,
      sgu_b_s.T, conv_w, lane_pad(a_log), lane_pad(dt_bias), a_log.reshape(N_HEADS, 1),
      dt_bias.reshape(N_HEADS, 1), norm_w.reshape(1, HEAD_DIM), w_out.astype(BF16),
      ln_g.reshape(1, d), ln_b.reshape(1, d))


def kernel(x, c, w_ada, b_ada, ffn1_w_gate, ffn1_w_up, ffn1_w_down, ln1_g, ln1_b, mix_w_in, sgu_ln_g,
           sgu_ln_b, sgu_w_s, sgu_b_s, gdn_conv_w, gdn_a_log, gdn_dt_bias, gdn_norm_w, mix_w_out,
           ln2_g, ln2_b, ffn2_w_gate, ffn2_w_up, ffn2_w_down, ln3_g, ln3_b):
    bsz, _, d = x.shape
    for l in range(w_ada.shape[0]):
        ada3 = _ada(c, w_ada, b_ada[l], l).reshape(bsz, 3 * N_SUB, d)
        width = N_HEADS * HEAD_DIM
        x, w2_gate, w2_up, w2_down, w_out, w_main, w_gates = _ffn(
            x, ada3, ffn1_w_gate[l], ffn1_w_up[l], ffn1_w_down[l], ln1_g[l], ln1_b[l], 0, layer=l,
            cast_weights=(ffn2_w_gate, ffn2_w_up, ffn2_w_down, mix_w_out),
            split_weight=(mix_w_in[l].astype(BF16), 6 * width + 2 * N_HEADS, 6 * width + LANES))
        x = _mixer(x, ada3, w_main, w_gates, sgu_ln_g[l], sgu_ln_b[l], sgu_w_s[l], sgu_b_s[l],
                   gdn_conv_w[l], gdn_a_log[l], gdn_dt_bias[l], gdn_norm_w[l], w_out,
                   ln2_g[l], ln2_b[l])
        x, = _ffn(x, ada3, w2_gate, w2_up, w2_down, ln3_g[l], ln3_b[l], 2)
    return x
```

```python
import functools

import jax
import jax.numpy as jnp
from jax import lax
from jax.experimental import pallas as pl
from jax.experimental.pallas import tpu as pltpu

F32 = jnp.float32
BF16 = jnp.bfloat16

N_SUB = 3
N_HEADS = 8
HEAD_DIM = 128
SGU_GROUPS = 8
SGU_BLOCK = 128
SGU_CHUNK = 64
CONV_WIDTH = 4
DEPTH = 1
MACARON_WEIGHT = 0.5
DEEPNORM_ALPHA = (2.0 * DEPTH) ** 0.25
LN_EPS = 1e-5
RMS_EPS = 1e-6
L2_EPS = 1e-6

LANES = 128
SUBLANES = 8
BF16_SUBLANES = 16
VMEM_LIMIT_BYTES = 56 * 1024 * 1024

MIX_TILE = 128
NEUMANN_BLOCK = 4
FFN_TILE = 1024
FFN_PARTS = 4


def _dot(a, b):
    return jnp.dot(a.astype(BF16), b.astype(BF16), preferred_element_type=F32)


def _layer_norm(y, gain, bias):
    mu = jnp.mean(y, axis=-1, keepdims=True)
    yc = y - mu
    var = jnp.mean(yc * yc, axis=-1, keepdims=True)
    return yc * lax.rsqrt(var + LN_EPS) * gain + bias


def _silu(x):
    return x * jax.nn.sigmoid(x)


def _softplus(x):
    return jnp.maximum(x, 0.0) + jnp.log1p(jnp.exp(-jnp.abs(x)))


def _const_spec(shape):
    nd = len(shape)
    return pl.BlockSpec(shape, lambda *_: (0,) * nd, pipeline_mode=pl.Buffered(1))


def _ada_kernel(c_ref, w_ref, b_ref, o_ref):
    cond = _silu(c_ref[...])
    o_ref[...] = _dot(cond, w_ref[0]) + b_ref[...]


def _ada(c, w_ada, b_ada, layer):
    bsz, d = c.shape
    n = w_ada.shape[2]
    return pl.pallas_call(
        _ada_kernel,
        grid=(n // d,),
        in_specs=[
            pl.BlockSpec((bsz, d), lambda j: (0, 0)),
            pl.BlockSpec((1, d, d), lambda j: (layer, 0, j)),
            pl.BlockSpec((1, d), lambda j: (0, j)),
        ],
        out_specs=pl.BlockSpec((bsz, d), lambda j: (0, j)),
        out_shape=jax.ShapeDtypeStruct((bsz, n), F32),
        compiler_params=pltpu.CompilerParams(dimension_semantics=("arbitrary",)),
        name="ada",
    )(c, w_ada, b_ada.reshape(1, n))


def _ffn_kernel(x_ref, ada_ref, wg_ref, wu_ref, wd_ref, lng_ref, lnb_ref, *rest, sub, n_cast, split_at):
    n_in = n_cast + (split_at is not None)
    o_ref = rest[n_in]
    for src_ref, dst_ref in zip(rest[:n_cast], rest[n_in + 1:]):
        dst_ref[...] = src_ref[0].astype(BF16)
    if split_at is not None:
        head_ref, tail_ref = rest[n_in + 1 + n_cast:]
        w = rest[n_cast][...].astype(F32)
        rows, n_head = head_ref.shape
        head_ref[:, :split_at] = w[:, :split_at].astype(BF16)
        head_ref[:, split_at:] = jnp.zeros((rows, n_head - split_at), BF16)
        tail_ref[...] = w[:, split_at:].astype(BF16)
    shift = ada_ref[0, 3 * sub:3 * sub + 1, :]
    scale = ada_ref[0, 3 * sub + 1:3 * sub + 2, :]
    gate = ada_ref[0, 3 * sub + 2:3 * sub + 3, :]
    tm = x_ref.shape[1]
    rows = [slice(i * tm // FFN_PARTS, (i + 1) * tm // FFN_PARTS) for i in range(FFN_PARTS)]
    xs = [x_ref[0, r, :] for r in rows]
    acts = []
    for x in xs:
        h = (x * (1.0 + scale) + shift).astype(BF16)
        g = jnp.dot(h, wg_ref[...], preferred_element_type=F32)
        u = jnp.dot(h, wu_ref[...], preferred_element_type=F32)
        acts.append((_silu(g) * u).astype(BF16))
    for r, x, a in zip(rows, xs, acts):
        f = jnp.dot(a, wd_ref[...], preferred_element_type=F32)
        y = DEEPNORM_ALPHA * x + (MACARON_WEIGHT * (1.0 + gate)) * f
        o_ref[0, r, :] = _layer_norm(y, lng_ref[...], lnb_ref[...])


def _ffn(x, ada3, w_gate, w_up, w_down, ln_g, ln_b, sub, layer=0, cast_weights=(), split_weight=None):
    bsz, seq, d = x.shape
    dff = w_gate.shape[1]
    tm = FFN_TILE
    tiles = seq // tm
    steps = bsz * tiles
    in_specs = [
        pl.BlockSpec((1, tm, d), lambda b, i: (b, i, 0)),
        pl.BlockSpec((1, 3 * N_SUB, d), lambda b, i: (b, 0, 0)),
        _const_spec((d, dff)),
        _const_spec((d, dff)),
        _const_spec((dff, d)),
        _const_spec((1, d)),
        _const_spec((1, d)),
    ]
    out_specs = [pl.BlockSpec((1, tm, d), lambda b, i: (b, i, 0))]
    out_shape = [jax.ShapeDtypeStruct((bsz, seq, d), F32)]
    for w in cast_weights:
        _, rows, cols = w.shape
        chunk = rows // steps
        assert chunk * steps == rows and chunk % BF16_SUBLANES == 0
        in_specs.append(pl.BlockSpec((1, chunk, cols), lambda b, i: (layer, b * tiles + i, 0)))
        out_specs.append(pl.BlockSpec((chunk, cols), lambda b, i: (b * tiles + i, 0)))
        out_shape.append(jax.ShapeDtypeStruct((rows, cols), BF16))
    extra = tuple(cast_weights)
    split_at = None
    if split_weight is not None:
        w, split_at, n_head = split_weight
        rows, cols = w.shape
        chunk = rows // steps
        assert chunk * steps == rows and chunk % BF16_SUBLANES == 0
        in_specs.append(pl.BlockSpec((chunk, cols), lambda b, i: (b * tiles + i, 0)))
        for n in (n_head, cols - split_at):
            out_specs.append(pl.BlockSpec((chunk, n), lambda b, i: (b * tiles + i, 0)))
            out_shape.append(jax.ShapeDtypeStruct((rows, n), BF16))
        extra += (w,)
    return pl.pallas_call(
        functools.partial(_ffn_kernel, sub=sub, n_cast=len(cast_weights), split_at=split_at),
        grid=(bsz, tiles),
        in_specs=in_specs,
        out_specs=out_specs,
        out_shape=out_shape,
        compiler_params=pltpu.CompilerParams(
            dimension_semantics=("arbitrary", "arbitrary"), vmem_limit_bytes=VMEM_LIMIT_BYTES),
        name=f"ffn{sub}",
    )(x, ada3, w_gate.astype(BF16), w_up.astype(BF16), w_down.astype(BF16),
      ln_g.reshape(1, d), ln_b.reshape(1, d), *extra)


def _block_diag(pair):
    t = pair.shape[0]
    pb = pair.astype(BF16)
    zero = jnp.zeros((t, t), BF16)
    return jnp.concatenate([jnp.concatenate([pb[:, :t], zero], axis=1),
                            jnp.concatenate([zero, pb[:, t:]], axis=1)], axis=0)


def _pair_dot(x, y):
    return jnp.dot(x.astype(BF16), _block_diag(y), preferred_element_type=F32)


def _unit_lower_inverse(a_list, row, col):
    t = a_list[0].shape[0]
    eye = (row == col).astype(F32)

    def same_block(n):
        return (row // n) == (col // n)

    a_diag = [jnp.where(same_block(NEUMANN_BLOCK), a, 0.0) for a in a_list]
    inv = [eye - m for m in a_diag]
    power = a_diag
    span = 2
    while span < NEUMANN_BLOCK:
        power = [_pair_dot(p, p) for p in power]
        inv = [i + _pair_dot(i, p) for i, p in zip(inv, power)]
        span *= 2
    n = NEUMANN_BLOCK
    while n < t:
        joins = same_block(2 * n) & jnp.logical_not(same_block(n))
        tx = [_pair_dot(i, jnp.where(joins, a, 0.0)) for i, a in zip(inv, a_list)]
        inv = [i - _pair_dot(txi, i) for i, txi in zip(inv, tx)]
        n *= 2
    return inv


def _causal_conv_silu(pre, tail, conv_w):
    rows = lax.broadcasted_iota(jnp.int32, tail.shape, 0)
    conv = pre * conv_w[CONV_WIDTH - 1:CONV_WIDTH, :]
    for s in range(1, CONV_WIDTH):
        rolled = pltpu.roll(pre, s, 0)
        head = jnp.where(rows < s, pltpu.roll(tail, s, 0), rolled[:SUBLANES, :])
        shifted = jnp.concatenate([head, rolled[SUBLANES:, :]], axis=0)
        conv = conv + shifted * conv_w[CONV_WIDTH - 1 - s:CONV_WIDTH - s, :]
    return _silu(conv)


def _mixer_kernel(x_ref, ada_ref, win_ref, wgates_ref, sgu_g_ref, sgu_b_ref, ws_ref, bs_ref, convw_ref,
                  alog_ref, dtb_ref, alogc_ref, dtbc_ref, normw_ref, wout_ref, lng_ref, lnb_ref, o_ref,
                  state_ref, tail_ref):
    nb, t, d = x_ref.shape
    width = N_HEADS * HEAD_DIM
    batches = range(nb)

    @pl.when(pl.program_id(0) == 0)
    def _():
        state_ref[...] = jnp.zeros_like(state_ref)
        tail_ref[...] = jnp.zeros_like(tail_ref)

    xs = [x_ref[b] for b in batches]
    h = jnp.concatenate(
        [(xs[b] * (1.0 + ada_ref[b, 4:5, :]) + ada_ref[b, 3:4, :]).astype(BF16) for b in batches], axis=0)
    o_u, o_v, o_qkv, o_z, o_ab = 0, width, 2 * width, 5 * width, 6 * width

    def project(lo, hi):
        return jnp.dot(h, win_ref[:, lo:hi], preferred_element_type=F32)

    row = lax.broadcasted_iota(jnp.int32, (t, t), 0)
    col = lax.broadcasted_iota(jnp.int32, (t, t), 1)
    lower = (row >= col).astype(F32)

    proj_uv = project(o_u, o_qkv)
    u_act = [jax.nn.gelu(proj_uv[b * t:(b + 1) * t, :width]) for b in batches]
    v_ln = [_layer_norm(jax.nn.gelu(proj_uv[b * t:(b + 1) * t, width:]), sgu_g_ref[...], sgu_b_ref[...])
            for b in batches]
    chunk_mask = (row // SGU_CHUNK) >= (col // SGU_CHUNK)
    y_a = [[] for _ in batches]
    for g in range(SGU_GROUPS):
        w = jnp.where(chunk_mask, ws_ref[g], 0.0)
        sl = slice(g * LANES, (g + 1) * LANES)
        mixed = _dot(w, jnp.concatenate([v_ln[b][:, sl] for b in batches], axis=1))
        for b in batches:
            y_a[b].append(u_act[b][:, sl] * (mixed[:, b * LANES:(b + 1) * LANES] + bs_ref[:, g:g + 1]))

    thirds = [[] for _ in batches]
    for third in range(3):
        cs = slice(third * width, (third + 1) * width)
        pre_all = project(o_qkv + third * width, o_qkv + (third + 1) * width)
        for b in batches:
            pre = pre_all[b * t:(b + 1) * t, :]
            tail = tail_ref[b, :, cs]
            tail_ref[b, :, cs] = pre[t - SUBLANES:, :]
            thirds[b].append(_causal_conv_silu(pre, tail, convw_ref[:, cs]))
    qkv = [jnp.concatenate(th, axis=1) for th in thirds]
    proj_zab = project(o_z, o_ab + LANES)
    proj_z = [proj_zab[b * t:(b + 1) * t, :width] for b in batches]
    beta_full, gc_col, gc_row = [], [], []
    for b in batches:
        ab = proj_zab[b * t:(b + 1) * t, width:]
        ab_t = ab.T
        g_col = -jnp.exp(alog_ref[...]) * _softplus(ab + dtb_ref[...])
        g_row = -jnp.exp(alogc_ref[...]) * _softplus(ab_t[0:N_HEADS, :] + dtbc_ref[...])
        beta_full.append(jax.nn.sigmoid(ab))
        gc_col.append(jnp.dot(lower, g_col, preferred_element_type=F32, precision=lax.Precision.HIGHEST))
        gc_row.append(lax.dot_general(g_row, lower, (((1,), (1,)), ((), ())),
                                      preferred_element_type=F32, precision=lax.Precision.HIGHEST))
    proj_gates = jnp.dot(h, wgates_ref[...], preferred_element_type=F32)

    hd2 = 2 * HEAD_DIM
    row2 = lax.broadcasted_iota(jnp.int32, (t, 2 * t), 0)
    col2 = lax.broadcasted_iota(jnp.int32, (t, 2 * t), 1) % t
    causal2 = row2 >= col2
    strict2 = row2 > col2
    pairs = [(b, j) for b in batches for j in range(N_HEADS // 2)]

    def per_half(f):
        return jnp.concatenate([f(0), f(1)], axis=1)

    def cols(x, first):
        return per_half(lambda i: jnp.broadcast_to(x[:, first + i:first + i + 1], (t, HEAD_DIM)))

    def l2_normalize(xp, scale):
        return xp * per_half(lambda i: jnp.broadcast_to(scale * lax.rsqrt(jnp.sum(
            jnp.square(xp[:, i * HEAD_DIM:(i + 1) * HEAD_DIM]), axis=-1, keepdims=True) + L2_EPS), (t, HEAD_DIM)))

    exp_gc_col = [jnp.exp(g) for g in gc_col]

    q, q_decayed, k_t, v_beta, k_beta, k_beta_g, decay, k_t_end, end_scale = [], [], [], [], [], [], [], [], []
    for b, j in pairs:
        hd = 2 * j
        lo = hd * HEAD_DIM
        qp = l2_normalize(qkv[b][:, lo:lo + hd2], HEAD_DIM ** -0.5)
        kp = l2_normalize(qkv[b][:, width + lo:width + lo + hd2], 1.0)
        vp = qkv[b][:, 2 * width + lo:2 * width + lo + hd2]
        beta = cols(beta_full[b], N_HEADS + hd)
        c = cols(gc_col[b], hd)
        r = per_half(lambda i: gc_row[b][hd + i:hd + i + 1, :])
        end = per_half(lambda i: jnp.broadcast_to(gc_row[b][hd + i:hd + i + 1, t - 1:t], (1, t)))
        ktp = per_half(lambda i: kp[:, i * HEAD_DIM:(i + 1) * HEAD_DIM].T)
        e_c = cols(exp_gc_col[b], hd)
        q.append(qp)
        q_decayed.append(qp * e_c)
        k_t.append(ktp)
        v_beta.append(vp * beta)
        k_beta.append(kp * beta)
        k_beta_g.append(kp * beta * e_c)
        decay.append(jnp.exp(jnp.where(causal2, c - r, -jnp.inf)))
        k_t_end.append(ktp * jnp.exp(end - r))
        end_scale.append(jnp.exp(end))

    n = range(len(pairs))
    a_mat = [jnp.where(strict2, _pair_dot(k_beta[p], k_t[p]) * decay[p], 0.0) for p in n]
    qk = [_pair_dot(q[p], k_t[p]) * decay[p] for p in n]
    t_inv = _unit_lower_inverse(a_mat, row2, col2)
    uw = [[_dot(t_inv[p][:, i * t:(i + 1) * t],
                jnp.concatenate([v_beta[p][:, i * HEAD_DIM:(i + 1) * HEAD_DIM],
                                 k_beta_g[p][:, i * HEAD_DIM:(i + 1) * HEAD_DIM]], axis=1))
           for i in range(2)] for p in n]
    u_val = [per_half(lambda i: uw[p][i][:, :HEAD_DIM]) for p in n]
    w_key = [per_half(lambda i: uw[p][i][:, HEAD_DIM:]) for p in n]
    state = [state_ref[p] for p in n]
    from_state = [jnp.dot(jnp.concatenate([w_key[p], q_decayed[p]], axis=0).astype(BF16),
                          _block_diag(state[p]), preferred_element_type=F32) for p in n]
    v_new = [u_val[p] - from_state[p][:t] for p in n]
    out = [from_state[p][t:] + _pair_dot(qk[p], v_new[p]) for p in n]
    for p in n:
        state_ref[p] = state[p] * end_scale[p] + _pair_dot(k_t_end[p], v_new[p])
    y_b = [[] for _ in batches]
    norm_w2 = per_half(lambda i: normw_ref[...])
    for p, (b, j) in enumerate(pairs):
        lo = 2 * j * HEAD_DIM
        o = out[p]
        o = o * per_half(lambda i: jnp.broadcast_to(lax.rsqrt(jnp.mean(
            jnp.square(o[:, i * HEAD_DIM:(i + 1) * HEAD_DIM]), axis=-1, keepdims=True) + RMS_EPS), (t, HEAD_DIM)))
        y_b[b].append(o * norm_w2 * _silu(proj_z[b][:, lo:lo + hd2]))

    merged = []
    for b in batches:
        gates = jax.nn.sigmoid(proj_gates[b * t:(b + 1) * t, :])
        merged.append((gates[:, :d] * jnp.concatenate(y_a[b], axis=1)
                       + gates[:, d:] * jnp.concatenate(y_b[b], axis=1)).astype(BF16))
    y = jnp.dot(jnp.concatenate(merged, axis=0), wout_ref[...], preferred_element_type=F32)
    for b in batches:
        res = DEEPNORM_ALPHA * xs[b] + (1.0 + ada_ref[b, 5:6, :]) * y[b * t:(b + 1) * t, :]
        o_ref[b] = _layer_norm(res, lng_ref[...], lnb_ref[...])


def _mixer(x, ada3, w_main, w_gates, sgu_ln_g, sgu_ln_b, sgu_w_s, sgu_b_s, conv_w, a_log, dt_bias, norm_w,
           w_out, ln_g, ln_b):
    bsz, seq, d = x.shape
    width = N_HEADS * HEAD_DIM
    assert MIX_TILE == HEAD_DIM and N_HEADS % 2 == 0
    n_main = 6 * width + LANES
    assert w_main.shape == (d, n_main) and w_gates.shape == (d, 2 * d)

    def lane_pad(p):
        return jnp.pad(p.reshape(1, N_HEADS), ((0, 0), (0, LANES - N_HEADS)))

    t = MIX_TILE
    return pl.pallas_call(
        _mixer_kernel,
        grid=(seq // t,),
        in_specs=[
            pl.BlockSpec((bsz, t, d), lambda i: (0, i, 0)),
            _const_spec((bsz, 3 * N_SUB, d)),
            _const_spec((d, n_main)),
            _const_spec((d, 2 * d)),
            _const_spec((1, width)),
            _const_spec((1, width)),
            _const_spec((SGU_GROUPS, SGU_BLOCK, SGU_BLOCK)),
            _const_spec((SGU_BLOCK, SGU_GROUPS)),
            _const_spec((CONV_WIDTH, 3 * width)),
            _const_spec((1, LANES)),
            _const_spec((1, LANES)),
            _const_spec((N_HEADS, 1)),
            _const_spec((N_HEADS, 1)),
            _const_spec((1, HEAD_DIM)),
            _const_spec((d, d)),
            _const_spec((1, d)),
            _const_spec((1, d)),
        ],
        out_specs=pl.BlockSpec((bsz, t, d), lambda i: (0, i, 0)),
        out_shape=jax.ShapeDtypeStruct((bsz, seq, d), F32),
        scratch_shapes=[
            pltpu.VMEM((bsz * N_HEADS // 2, HEAD_DIM, 2 * HEAD_DIM), F32),
            pltpu.VMEM((bsz, SUBLANES, 3 * width), F32),
        ],
        compiler_params=pltpu.CompilerParams(
            dimension_semantics=("arbitrary",), vmem_limit_bytes=VMEM_LIMIT_BYTES),
        name="mixer",
    )(x, ada3, w_main, w_gates, sgu_ln_g.reshape(1, width), sgu_ln_b.reshape(1, width), sgu_w_s,
      sgu_b_s.T, conv_w, lane_pad(a_log), lane_pad(dt_bias), a_log.reshape(N_HEADS, 1),
      dt_bias.reshape(N_HEADS, 1), norm_w.reshape(1, HEAD_DIM), w_out.astype(BF16),
      ln_g.reshape(1, d), ln_b.reshape(1, d))


def kernel(x, c, w_ada, b_ada, ffn1_w_gate, ffn1_w_up, ffn1_w_down, ln1_g, ln1_b, mix_w_in, sgu_ln_g,
           sgu_ln_b, sgu_w_s, sgu_b_s, gdn_conv_w, gdn_a_log, gdn_dt_bias, gdn_norm_w, mix_w_out,
           ln2_g, ln2_b, ffn2_w_gate, ffn2_w_up, ffn2_w_down, ln3_g, ln3_b):
    bsz, _, d = x.shape
    for l in range(w_ada.shape[0]):
        ada3 = _ada(c, w_ada, b_ada[l], l).reshape(bsz, 3 * N_SUB, d)
        width = N_HEADS * HEAD_DIM
        x, w2_gate, w2_up, w2_down, w_out, w_main, w_gates = _ffn(
            x, ada3, ffn1_w_gate[l], ffn1_w_up[l], ffn1_w_down[l], ln1_g[l], ln1_b[l], 0, layer=l,
            cast_weights=(ffn2_w_gate, ffn2_w_up, ffn2_w_down, mix_w_out),
            split_weight=(mix_w_in[l].astype(BF16), 6 * width + 2 * N_HEADS, 6 * width + LANES))
        x = _mixer(x, ada3, w_main, w_gates, sgu_ln_g[l], sgu_ln_b[l], sgu_w_s[l], sgu_b_s[l],
                   gdn_conv_w[l], gdn_a_log[l], gdn_dt_bias[l], gdn_norm_w[l], w_out,
                   ln2_g[l], ln2_b[l])
        x, = _ffn(x, ada3, w2_gate, w2_up, w2_down, ln3_g[l], ln3_b[l], 2)
    return x
```
